```python
import jax, jax.numpy as jnp
from jax import lax
import numpy as np

D_MODEL = 2048
BATCH = 1
SEQ = 16384
DEPTH = 4
DEC_BATCH = 32
DEC_SEQ = 32
PAST_LEN = 1024

CHUNK = 64
N_BRANCH = 4
BR_WIDTH = D_MODEL // 2
GMLP_BLOCK = 128
GMLP_GROUPS = 8
GMLP_GC = BR_WIDTH // GMLP_GROUPS
CONV_WIDTH = 31
HEAD_DIM = 64
SWA_HEADS = BR_WIDTH // HEAD_DIM
SWA_KV_HEADS = 4
SWA_WINDOW = 128
SWA_PREV = SWA_WINDOW // CHUNK
BAND_HEADS = BR_WIDTH // HEAD_DIM
BAND_PREV = 8
MAX_REL = 256
ROPE_THETA = 500000.0
ROT_DIM = HEAD_DIM // 4
D_FF = 4 * D_MODEL
DEEPNORM_ALPHA = (2 * DEPTH) ** 0.25
DEEPNORM_BETA = (8 * DEPTH) ** -0.25
LN_EPS = 1e-5
NEG_INF = -1e30
IN_SIZES = (BR_WIDTH, BR_WIDTH, 2 * BR_WIDTH, SWA_HEADS * HEAD_DIM, SWA_KV_HEADS * HEAD_DIM,
            SWA_KV_HEADS * HEAD_DIM, BAND_HEADS * HEAD_DIM, BAND_HEADS * HEAD_DIM,
            BAND_HEADS * HEAD_DIM, N_BRANCH * D_MODEL)
N_IN = sum(IN_SIZES)

kernel_name = 'hybrid_streaming_encoder_step'


def layer_norm(x, g, b):
    xf = x.astype(jnp.float32)
    mu = jnp.mean(xf, axis=-1, keepdims=True)
    var = jnp.mean(jnp.square(xf - mu), axis=-1, keepdims=True)
    y = (xf - mu) * lax.rsqrt(var + LN_EPS) * g.astype(jnp.float32) + b.astype(jnp.float32)
    return y.astype(x.dtype)


def partial_rope(x, pos):
    half = ROT_DIM // 2
    inv_freq = jnp.power(jnp.float32(ROPE_THETA), -jnp.arange(half, dtype=jnp.float32) / half)
    ang = pos.astype(jnp.float32)[:, None] * inv_freq[None, :]
    cos = jnp.cos(ang)[None, :, None, :]
    sin = jnp.sin(ang)[None, :, None, :]
    xr = x[..., :ROT_DIM].astype(jnp.float32)
    x1, x2 = xr[..., :half], xr[..., half:]
    rot = jnp.concatenate([x1 * cos - x2 * sin, x2 * cos + x1 * sin], axis=-1).astype(x.dtype)
    return jnp.concatenate([rot, x[..., ROT_DIM:]], axis=-1)


def rel_bias(table, n_q, n_k, offset):
    d = offset + jnp.arange(n_q)[:, None] - jnp.arange(n_k)[None, :]
    idx = jnp.clip(d, -MAX_REL, MAX_REL) + MAX_REL
    return table[:, idx]


def block_attend(q, k, v, valid, bias, sink):
    b, n_q, n_h, hd = q.shape
    n_kv = k.shape[2]
    grp = n_h // n_kv
    qg = q.reshape(b, n_q, n_kv, grp, hd)
    s = jnp.einsum('bqkgd,bskd->bkgqs', qg, k, preferred_element_type=jnp.float32) * (hd ** -0.5)
    if bias is not None:
        s = s + bias.astype(jnp.float32).reshape(n_kv, grp, n_q, -1)
    if valid is not None:
        s = jnp.where(valid, s, NEG_INF)
    m = jnp.max(s, axis=-1, keepdims=True)
    if sink is not None:
        sk = sink.astype(jnp.float32).reshape(n_kv, grp, 1, 1)
        m = jnp.maximum(m, sk)
    p = jnp.exp(s - m)
    den = jnp.sum(p, axis=-1, keepdims=True)
    if sink is not None:
        den = den + jnp.exp(sk - m)
    p = (p / den).astype(v.dtype)
    o = jnp.einsum('bkgqs,bskd->bqkgd', p, v)
    return o.reshape(b, n_q, n_h, hd)


def band_attend_prompt(q, k, v, n_prev, bias, sink):
    b, t = q.shape[0], q.shape[1]
    pad = n_prev * CHUNK
    k_pad = jnp.concatenate([jnp.zeros((b, pad) + k.shape[2:], k.dtype), k], axis=1)
    v_pad = jnp.concatenate([jnp.zeros((b, pad) + v.shape[2:], v.dtype), v], axis=1)
    band = pad + CHUNK

    def one_chunk(c):
        start = c * CHUNK
        qc = lax.dynamic_slice_in_dim(q, start, CHUNK, axis=1)
        kc = lax.dynamic_slice_in_dim(k_pad, start, band, axis=1)
        vc = lax.dynamic_slice_in_dim(v_pad, start, band, axis=1)
        valid = (start - pad + jnp.arange(band)) >= 0
        return block_attend(qc, kc, vc, valid, bias, sink)

    out = lax.map(one_chunk, jnp.arange(t // CHUNK))
    out = jnp.moveaxis(out, 0, 1).reshape(q.shape)
    return out, k_pad[:, -pad:], v_pad[:, -pad:]


def band_attend_sample(q, k, v, k_cache, v_cache, bias_table, sink):
    rows = k_cache.shape[1]
    t = q.shape[1]
    kb = jnp.concatenate([k_cache, k], axis=1)
    vb = jnp.concatenate([v_cache, v], axis=1)
    bias = None if bias_table is None else rel_bias(bias_table, t, rows + t, rows)
    out = block_attend(q, kb, vb, None, bias, sink)
    return out, kb[:, -rows:], vb[:, -rows:]


def split_columns(proj):
    parts = []
    start = 0
    for size in IN_SIZES:
        parts.append(proj[..., start:start + size])
        start += size
    return parts


def token_mixing(h, pos, cache, lp):
    b, t, _ = h.shape
    proj = jnp.einsum('btd,dn->btn', h, lp['w_in'])
    a_u, a_v, b_in, c_q, c_k, c_v, d_q, d_k, d_v, gate_logits = split_columns(proj)

    u = jax.nn.gelu(a_u)
    vn = layer_norm(jax.nn.gelu(a_v), lp['gmlp_ln_g'], lp['gmlp_ln_b'])
    blk = min(t, GMLP_BLOCK)
    causal = jnp.tril(jnp.ones((blk, blk), dtype=bool))
    ws = jnp.where(causal, lp['w_spatial'][:, :blk, :blk], 0.0)
    vb = vn.reshape(b, t // blk, blk, GMLP_GROUPS, GMLP_GC)
    mixed = jnp.einsum('gts,bnsgc->bntgc', ws, vb) + lp['b_spatial'][:, :blk].T[None, None, :, :, None]
    out_a = u * mixed.reshape(b, t, BR_WIDTH)

    glu = b_in[..., :BR_WIDTH] * jax.nn.sigmoid(b_in[..., BR_WIDTH:])
    if cache is None:
        hist = jnp.zeros((b, CONV_WIDTH - 1, BR_WIDTH), glu.dtype)
    else:
        hist = cache[0]
    xc = jnp.concatenate([hist, glu], axis=1)
    y = lax.conv_general_dilated(xc, lp['conv_w'][:, None, :], (1,), 'VALID',
                                 dimension_numbers=('NWC', 'WIO', 'NWC'),
                                 feature_group_count=BR_WIDTH) + lp['conv_b']
    out_b = jax.nn.silu(layer_norm(y, lp['conv_ln_g'], lp['conv_ln_b']))
    new_conv = xc[:, -(CONV_WIDTH - 1):]

    q = partial_rope(c_q.reshape(b, t, SWA_HEADS, HEAD_DIM), pos)
    k = partial_rope(c_k.reshape(b, t, SWA_KV_HEADS, HEAD_DIM), pos)
    v = c_v.reshape(b, t, SWA_KV_HEADS, HEAD_DIM)
    if cache is None:
        out_c, swa_k, swa_v = band_attend_prompt(q, k, v, SWA_PREV, None, lp['swa_sinks'])
    else:
        out_c, swa_k, swa_v = band_attend_sample(q, k, v, cache[1], cache[2], None, lp['swa_sinks'])

    q = d_q.reshape(b, t, BAND_HEADS, HEAD_DIM)
    k = d_k.reshape(b, t, BAND_HEADS, HEAD_DIM)
    v = d_v.reshape(b, t, BAND_HEADS, HEAD_DIM)
    if cache is None:
        off = BAND_PREV * CHUNK
        bias = rel_bias(lp['band_rel_bias'], CHUNK, off + CHUNK, off)
        out_d, band_k, band_v = band_attend_prompt(q, k, v, BAND_PREV, bias, None)
    else:
        out_d, band_k, band_v = band_attend_sample(q, k, v, cache[3], cache[4], lp['band_rel_bias'], None)

    branches = jnp.stack([out_a, out_b, out_c.reshape(b, t, BR_WIDTH), out_d.reshape(b, t, BR_WIDTH)], axis=2)
    gates = jax.nn.sigmoid(gate_logits.reshape(b, t, N_BRANCH, D_MODEL))
    merged = jnp.sum(jnp.einsum('btnc,ncd->btnd', branches, lp['w_branch']) * gates, axis=2)
    out = jnp.einsum('btd,de->bte', merged, lp['w_out'])
    if cache is None:
        states = (new_conv, swa_k, swa_v, band_k, band_v)
    else:
        states = (new_conv, swa_k, swa_v, band_k, band_v, vn)
    return out, states


def run_trunk(x, c, pos, caches, params):
    (w_ada, b_ada, w_in, gmlp_ln_g, gmlp_ln_b, w_spatial, b_spatial, conv_w, conv_b, conv_ln_g,
     conv_ln_b, swa_sinks, band_rel_bias, w_branch, w_out, ln1_g, ln1_b, w_ff1, w_ff2, ln2_g, ln2_b) = params
    cond = jax.nn.silu(c)
    per_layer = []
    for l in range(DEPTH):
        mod = jnp.einsum('bd,de->be', cond, w_ada[l]) + b_ada[l]
        sh1, sc1, g1, sh2, sc2, g2 = [m[:, None, :] for m in jnp.split(mod, 6, axis=-1)]
        lp = {'w_in': w_in[l], 'gmlp_ln_g': gmlp_ln_g[l], 'gmlp_ln_b': gmlp_ln_b[l],
              'w_spatial': w_spatial[l], 'b_spatial': b_spatial[l], 'conv_w': conv_w[l],
              'conv_b': conv_b[l], 'conv_ln_g': conv_ln_g[l], 'conv_ln_b': conv_ln_b[l],
              'swa_sinks': swa_sinks[l], 'band_rel_bias': band_rel_bias[l],
              'w_branch': w_branch[l], 'w_out': w_out[l]}
        cache_l = None if caches is None else [cc[l] for cc in caches]
        h = x * (1.0 + sc1) + sh1
        o, st = token_mixing(h, pos, cache_l, lp)
        x = layer_norm(DEEPNORM_ALPHA * x + (1.0 + g1) * o, ln1_g[l], ln1_b[l])
        h = x * (1.0 + sc2) + sh2
        hid = jnp.square(jax.nn.relu(jnp.einsum('btd,df->btf', h, w_ff1[l])))
        f = jnp.einsum('btf,fd->btd', hid, w_ff2[l])
        x = layer_norm(DEEPNORM_ALPHA * x + (1.0 + g2) * f, ln2_g[l], ln2_b[l])
        per_layer.append(st)
    stacked = [jnp.stack(s, axis=0) for s in zip(*per_layer)]
    return x, stacked


def setup_inputs(seed: int = 0) -> dict:
    key = jax.random.key(seed)
    ks = jax.random.split(key, 32)

    def nrm(k, shape, scale):
        return jax.random.normal(k, shape, jnp.float32) * scale

    def gain(k, shape):
        return 1.0 + nrm(k, shape, 0.02)

    swa_rows = min(SWA_PREV * CHUNK, PAST_LEN)
    band_rows = min(BAND_PREV * CHUNK, PAST_LEN)
    return {
        'x_prompt': nrm(ks[0], (BATCH, SEQ, D_MODEL), 1.0),
        'x_sample': nrm(ks[1], (DEC_BATCH, DEC_SEQ, D_MODEL), 1.0),
        'state_conv': nrm(ks[2], (DEPTH, DEC_BATCH, CONV_WIDTH - 1, BR_WIDTH), 0.5),
        'cache_swa_k': nrm(ks[3], (DEPTH, DEC_BATCH, swa_rows, SWA_KV_HEADS, HEAD_DIM), 1.0),
        'cache_swa_v': nrm(ks[4], (DEPTH, DEC_BATCH, swa_rows, SWA_KV_HEADS, HEAD_DIM), 1.0),
        'cache_band_k': nrm(ks[5], (DEPTH, DEC_BATCH, band_rows, BAND_HEADS, HEAD_DIM), 1.0),
        'cache_band_v': nrm(ks[6], (DEPTH, DEC_BATCH, band_rows, BAND_HEADS, HEAD_DIM), 1.0),
        'c_prompt': nrm(ks[7], (BATCH, D_MODEL), 1.0),
        'c_sample': nrm(ks[8], (DEC_BATCH, D_MODEL), 1.0),
        'w_ada': nrm(ks[9], (DEPTH, D_MODEL, 6 * D_MODEL), 0.1 * D_MODEL ** -0.5),
        'b_ada': nrm(ks[10], (DEPTH, 6 * D_MODEL), 0.02),
        'w_in': nrm(ks[11], (DEPTH, D_MODEL, N_IN), D_MODEL ** -0.5),
        'gmlp_ln_g': gain(ks[12], (DEPTH, BR_WIDTH)),
        'gmlp_ln_b': nrm(ks[13], (DEPTH, BR_WIDTH), 0.02),
        'w_spatial': nrm(ks[14], (DEPTH, GMLP_GROUPS, GMLP_BLOCK, GMLP_BLOCK), GMLP_BLOCK ** -0.5),
        'b_spatial': gain(ks[15], (DEPTH, GMLP_GROUPS, GMLP_BLOCK)),
        'conv_w': nrm(ks[16], (DEPTH, CONV_WIDTH, BR_WIDTH), CONV_WIDTH ** -0.5),
        'conv_b': nrm(ks[17], (DEPTH, BR_WIDTH), 0.02),
        'conv_ln_g': gain(ks[18], (DEPTH, BR_WIDTH)),
        'conv_ln_b': nrm(ks[19], (DEPTH, BR_WIDTH), 0.02),
        'swa_sinks': nrm(ks[20], (DEPTH, SWA_HEADS), 0.5),
        'band_rel_bias': nrm(ks[21], (DEPTH, BAND_HEADS, 2 * MAX_REL + 1), 0.2),
        'w_branch': nrm(ks[22], (DEPTH, N_BRANCH, BR_WIDTH, D_MODEL), DEEPNORM_BETA * BR_WIDTH ** -0.5),
        'w_out': nrm(ks[23], (DEPTH, D_MODEL, D_MODEL), DEEPNORM_BETA * D_MODEL ** -0.5),
        'ln1_g': gain(ks[24], (DEPTH, D_MODEL)),
        'ln1_b': nrm(ks[25], (DEPTH, D_MODEL), 0.02),
        'w_ff1': nrm(ks[26], (DEPTH, D_MODEL, D_FF), D_MODEL ** -0.5),
        'w_ff2': nrm(ks[27], (DEPTH, D_FF, D_MODEL), DEEPNORM_BETA * D_FF ** -0.5),
        'ln2_g': gain(ks[28], (DEPTH, D_MODEL)),
        'ln2_b': nrm(ks[29], (DEPTH, D_MODEL), 0.02),
    }


def reference(x_prompt, x_sample, state_conv, cache_swa_k, cache_swa_v, cache_band_k, cache_band_v,
              c_prompt, c_sample, w_ada, b_ada, w_in, gmlp_ln_g, gmlp_ln_b, w_spatial, b_spatial,
              conv_w, conv_b, conv_ln_g, conv_ln_b, swa_sinks, band_rel_bias, w_branch, w_out,
              ln1_g, ln1_b, w_ff1, w_ff2, ln2_g, ln2_b):
    params = (w_ada, b_ada, w_in, gmlp_ln_g, gmlp_ln_b, w_spatial, b_spatial, conv_w, conv_b,
              conv_ln_g, conv_ln_b, swa_sinks, band_rel_bias, w_branch, w_out, ln1_g, ln1_b,
              w_ff1, w_ff2, ln2_g, ln2_b)
    pos_prompt = jnp.arange(x_prompt.shape[1], dtype=jnp.int32)
    pos_sample = PAST_LEN + jnp.arange(x_sample.shape[1], dtype=jnp.int32)
    y_prompt, st_p = run_trunk(x_prompt, c_prompt, pos_prompt, None, params)
    y_sample, st_s = run_trunk(x_sample, c_sample, pos_sample,
                               (state_conv, cache_swa_k, cache_swa_v, cache_band_k, cache_band_v), params)
    conv_p, swa_k_p, swa_v_p, band_k_p, band_v_p = st_p
    conv_s, swa_k_s, swa_v_s, band_k_s, band_v_s, gmlp_v_s = st_s
    return (y_prompt, y_sample, conv_p, conv_s, swa_k_p, swa_v_p, swa_k_s, swa_v_s,
            band_k_p, band_v_p, band_k_s, band_v_s, gmlp_v_s)
```

```python
import functools

import jax
import jax.numpy as jnp
import numpy as np
from jax import lax
from jax.experimental import pallas as pl
from jax.experimental.pallas import tpu as pltpu

BF = jnp.bfloat16
F32 = jnp.float32

CHUNK = 64
HEAD_DIM = 64
GMLP_BLOCK = 128
GMLP_GROUPS = 8
CONV_WIDTH = 31
SWA_KV_HEADS = 4
SWA_PREV = 2
BAND_PREV = 8
MAX_REL = 256
PAST_LEN = 1024
ROPE_THETA = 500000.0
ROT_DIM = HEAD_DIM // 4
LN_EPS = 1e-5
NEG_INF = -1e30

LANES = 128
CONV_HALO = 32
VMEM_LIMIT = 56 * 1024 * 1024


def _params(*sem):
    return pltpu.CompilerParams(dimension_semantics=sem, vmem_limit_bytes=VMEM_LIMIT)


def _ln(x, g, b):
    mu = jnp.mean(x, axis=-1, keepdims=True)
    xc = x - mu
    var = jnp.mean(xc * xc, axis=-1, keepdims=True)
    return xc * lax.rsqrt(var + LN_EPS) * g + b


def _sigmoid(x):
    return jax.nn.sigmoid(x)


def _mm_kernel(*refs, act, pre, has_bias):
    if has_bias:
        a_ref, w_ref, b_ref, o_ref = refs
    else:
        a_ref, w_ref, o_ref = refs
    a = a_ref[...]
    if pre == "silu":
        a = a * _sigmoid(a)
    acc = jnp.dot(a.astype(BF), w_ref[...], preferred_element_type=F32)
    if has_bias:
        acc = acc + b_ref[...]
    if act == "gelu":
        acc = jax.nn.gelu(acc)
    elif act == "sigmoid":
        acc = _sigmoid(acc)
    elif act == "relu2":
        acc = jnp.square(jnp.maximum(acc, 0.0))
    elif act == "glu":
        half = acc.shape[1] // 2
        acc = acc[:, :half] * _sigmoid(acc[:, half:])
    o_ref[...] = acc.astype(o_ref.dtype)


def _mm(a, w, *, name, act=None, pre=None, bias=None, out_dtype=BF, bm=1024, bn=1024):
    m, k = a.shape
    n = w.shape[1]
    bm, bn = min(bm, m), min(bn, n)
    assert m % bm == 0 and n % bn == 0, (m, n, bm, bn)
    on = bn // 2 if act == "glu" else bn
    in_specs = [pl.BlockSpec((bm, k), lambda i, j: (i, 0)),
                pl.BlockSpec((k, bn), lambda i, j: (0, j))]
    args = [a, w]
    if bias is not None:
        in_specs.append(pl.BlockSpec((1, bn), lambda i, j: (0, j)))
        args.append(bias)
    return pl.pallas_call(
        functools.partial(_mm_kernel, act=act, pre=pre, has_bias=bias is not None),
        grid=(m // bm, n // bn),
        in_specs=in_specs,
        out_specs=pl.BlockSpec((bm, on), lambda i, j: (i, j)),
        out_shape=jax.ShapeDtypeStruct((m, (n // bn) * on), out_dtype),
        compiler_params=_params("parallel", "arbitrary"),
        name=name,
    )(*args)


def _mm_ln_kernel(a_ref, w_ref, x_ref, g_ref, lng_ref, lnb_ref, sc_ref, sh_ref,
                  xo_ref, ho_ref, acc_ref, *, alpha):
    k = pl.program_id(1)

    @pl.when(k == 0)
    def _():
        acc_ref[...] = jnp.zeros_like(acc_ref)

    acc_ref[...] += jnp.dot(a_ref[...], w_ref[...], preferred_element_type=F32)

    @pl.when(k == pl.num_programs(1) - 1)
    def _():
        y = alpha * x_ref[...] + (1.0 + g_ref[...]) * acc_ref[...]
        xn = _ln(y, lng_ref[...], lnb_ref[...])
        xo_ref[...] = xn
        ho_ref[...] = (xn * (1.0 + sc_ref[...]) + sh_ref[...]).astype(ho_ref.dtype)


def _row_spec(arr, bm, d):
    if arr.shape[0] == 1:
        return pl.BlockSpec((1, d), lambda i, k: (0, 0))
    return pl.BlockSpec((bm, d), lambda i, k: (i, 0))


def _mm_ln(a, w, x, gate, ln_g, ln_b, scale, shift, *, alpha, name, bm=512, bk=512):
    m, kdim = a.shape
    d = w.shape[1]
    if gate.shape[0] != 1:
        bm //= 2
    bm, bk = min(bm, m), min(bk, kdim)
    assert m % bm == 0 and kdim % bk == 0
    const = pl.BlockSpec((1, d), lambda i, k: (0, 0))
    rows = pl.BlockSpec((bm, d), lambda i, k: (i, 0))
    return pl.pallas_call(
        functools.partial(_mm_ln_kernel, alpha=alpha),
        grid=(m // bm, kdim // bk),
        in_specs=[pl.BlockSpec((bm, bk), lambda i, k: (i, k)),
                  pl.BlockSpec((bk, d), lambda i, k: (k, 0)),
                  rows, _row_spec(gate, bm, d), const, const,
                  _row_spec(scale, bm, d), _row_spec(shift, bm, d)],
        out_specs=[rows, rows],
        out_shape=[jax.ShapeDtypeStruct((m, d), F32), jax.ShapeDtypeStruct((m, d), BF)],
        scratch_shapes=[pltpu.VMEM((bm, d), F32)],
        compiler_params=_params("parallel", "arbitrary"),
        name=name,
    )(a, w, x, gate, ln_g, ln_b, scale, shift)


def _mod_kernel(x_ref, sc_ref, sh_ref, o_ref):
    o_ref[...] = (x_ref[...] * (1.0 + sc_ref[...]) + sh_ref[...]).astype(o_ref.dtype)


def _modulate(x, scale, shift, *, name, bm=512):
    m, d = x.shape
    bm = min(bm, m)
    rows = pl.BlockSpec((bm, d), lambda i: (i, 0))

    def spec(arr):
        if arr.shape[0] == 1:
            return pl.BlockSpec((1, d), lambda i: (0, 0))
        return rows

    return pl.pallas_call(
        _mod_kernel, grid=(m // bm,),
        in_specs=[rows, spec(scale), spec(shift)],
        out_specs=rows,
        out_shape=jax.ShapeDtypeStruct((m, d), BF),
        compiler_params=_params("parallel"),
        name=name,
    )(x, scale, shift)


def _rope_kernel(x_ref, c_ref, sa_ref, sb_ref, o_ref):
    half = ROT_DIM // 2
    c, sa, sb = c_ref[...], sa_ref[...], sb_ref[...]
    for s in range(x_ref.shape[1] // LANES):
        x = x_ref[:, s * LANES:(s + 1) * LANES].astype(F32)
        up = pltpu.roll(x, LANES - half, axis=1)
        dn = pltpu.roll(x, half, axis=1)
        o_ref[:, s * LANES:(s + 1) * LANES] = (x * c + up * sa + dn * sb).astype(o_ref.dtype)


def _rope(slab, width, tables, *, name, bm=512):
    m = slab.shape[0]
    bm = min(bm, m)
    tab = pl.BlockSpec((bm, LANES), lambda i: (i, 0))
    return pl.pallas_call(
        _rope_kernel, grid=(m // bm,),
        in_specs=[pl.BlockSpec((bm, width), lambda i: (i, 0)), tab, tab, tab],
        out_specs=pl.BlockSpec((bm, width), lambda i: (i, 0)),
        out_shape=jax.ShapeDtypeStruct((m, width), BF),
        compiler_params=_params("parallel"),
        name=name,
    )(slab, *tables)


def _rope_tables(pos):
    half = ROT_DIM // 2
    inv_freq = jnp.power(jnp.float32(ROPE_THETA), -jnp.arange(half, dtype=F32) / half)
    ang = pos.astype(F32)[:, None] * inv_freq[None, :]
    cos, sin = jnp.cos(ang), jnp.sin(ang)
    t = pos.shape[0]
    ones = jnp.ones((t, HEAD_DIM - ROT_DIM), F32)
    zeros = jnp.zeros((t, HEAD_DIM - ROT_DIM), F32)
    zh = jnp.zeros((t, half), F32)
    c = jnp.concatenate([cos, cos, ones], axis=1)
    sa = jnp.concatenate([-sin, zh, zeros], axis=1)
    sb = jnp.concatenate([zh, sin, zeros], axis=1)
    rep = LANES // HEAD_DIM
    return tuple(jnp.tile(a, (1, rep)) for a in (c, sa, sb))


def _gmlp_kernel(ug_ref, ws_ref, bmap_ref, lng_ref, lnb_ref, o_ref, *vn_refs, blk):
    width = o_ref.shape[1]
    gc = width // GMLP_GROUPS
    t = lax.broadcasted_iota(jnp.int32, (GMLP_BLOCK, GMLP_BLOCK), 0)
    s = lax.broadcasted_iota(jnp.int32, (GMLP_BLOCK, GMLP_BLOCK), 1)
    mask = (s <= t) & (s >= (t // blk) * blk)
    ws = [jnp.where(mask, ws_ref[g], 0.0).astype(BF) for g in range(GMLP_GROUPS)]
    bmap = bmap_ref[...]
    for r in range(o_ref.shape[0] // GMLP_BLOCK):
        rows = slice(r * GMLP_BLOCK, (r + 1) * GMLP_BLOCK)
        vn = _ln(ug_ref[rows, width:].astype(F32), lng_ref[...], lnb_ref[...])
        if vn_refs:
            vn_refs[0][rows, :] = vn
        vnb = vn.astype(BF)
        for g in range(GMLP_GROUPS):
            cols = slice(g * gc, (g + 1) * gc)
            mixed = jnp.dot(ws[g], vnb[:, cols], preferred_element_type=F32)
            u = ug_ref[rows, cols].astype(F32)
            o_ref[rows, cols] = (u * (mixed + bmap[:, cols])).astype(o_ref.dtype)


def _gmlp(ug, ws, bmap, ln_g, ln_b, *, blk, emit_vn, name, bm=512):
    m, two_w = ug.shape
    width = two_w // 2
    bm = min(bm, m)
    full = lambda *shape: pl.BlockSpec(shape, lambda i: (0,) * len(shape))
    out_specs = [pl.BlockSpec((bm, width), lambda i: (i, 0))]
    out_shape = [jax.ShapeDtypeStruct((m, width), BF)]
    if emit_vn:
        out_specs.append(pl.BlockSpec((bm, width), lambda i: (i, 0)))
        out_shape.append(jax.ShapeDtypeStruct((m, width), F32))
    return pl.pallas_call(
        functools.partial(_gmlp_kernel, blk=blk), grid=(m // bm,),
        in_specs=[pl.BlockSpec((bm, two_w), lambda i: (i, 0)),
                  full(GMLP_GROUPS, GMLP_BLOCK, GMLP_BLOCK), full(GMLP_BLOCK, width),
                  full(1, width), full(1, width)],
        out_specs=out_specs, out_shape=out_shape,
        compiler_params=_params("parallel"),
        name=name,
    )(ug, ws, bmap, ln_g, ln_b)


CONV_ROWS = 16


def _conv_kernel(cur_ref, prev_ref, w_ref, cb_ref, lng_ref, lnb_ref, o_ref, xc_ref, *, zero_first):
    rows = cur_ref.shape[0]
    prev = prev_ref[...].astype(F32)
    if zero_first:
        prev = jnp.where(pl.program_id(0) == 0, 0.0, prev)
    xc_ref[0:CONV_HALO, :] = prev
    xc_ref[CONV_HALO:CONV_HALO + rows, :] = cur_ref[...].astype(F32)
    first = CONV_HALO - (CONV_WIDTH - 1)
    for c in range(rows // CONV_ROWS):
        base = c * CONV_ROWS + first
        acc = jnp.zeros((CONV_ROWS, cur_ref.shape[1]), F32)
        for j in range(CONV_WIDTH):
            acc = acc + w_ref[j:j + 1, :] * xc_ref[base + j:base + j + CONV_ROWS, :]
        y = _ln(acc + cb_ref[...], lng_ref[...], lnb_ref[...])
        o_ref[c * CONV_ROWS:(c + 1) * CONV_ROWS, :] = (y * _sigmoid(y)).astype(o_ref.dtype)


def _conv(glu, prev, conv_w, conv_b, ln_g, ln_b, *, rows, prev_is_glu, name):
    m, width = glu.shape
    rows = min(rows, m)
    step = rows // CONV_HALO
    if prev_is_glu:
        prev_spec = pl.BlockSpec((CONV_HALO, width), lambda i: (jnp.maximum(i * step - 1, 0), 0))
    else:
        prev_spec = pl.BlockSpec((CONV_HALO, width), lambda i: (i, 0))
    full = lambda *shape: pl.BlockSpec(shape, lambda i: (0,) * len(shape))
    return pl.pallas_call(
        functools.partial(_conv_kernel, zero_first=prev_is_glu), grid=(m // rows,),
        in_specs=[pl.BlockSpec((rows, width), lambda i: (i, 0)), prev_spec,
                  full(CONV_WIDTH, width), full(1, width), full(1, width), full(1, width)],
        out_specs=pl.BlockSpec((rows, width), lambda i: (i, 0)),
        out_shape=jax.ShapeDtypeStruct((m, width), BF),
        scratch_shapes=[pltpu.VMEM((CONV_HALO + rows, width), F32)],
        compiler_params=_params("parallel"),
        name=name,
    )(glu, prev, conv_w, conv_b, ln_g, ln_b)


def _half_mask(shape, half):
    lane = lax.broadcasted_iota(jnp.int32, shape, 1)
    return (lane >= HEAD_DIM) if half else (lane < HEAD_DIM)


def _head_scores(q_slab, q_half, k_half, k_slabs):
    if q_half != k_half:
        q_slab = pltpu.roll(q_slab, HEAD_DIM, axis=1)
    qm = jnp.where(_half_mask(q_slab.shape, k_half), q_slab, jnp.zeros_like(q_slab))
    return [lax.dot_general(qm, ks, (((1,), (1,)), ((), ())), preferred_element_type=F32)
            for ks in k_slabs]


def _softmax_pv(scores, v_slabs, sink):
    m = functools.reduce(jnp.maximum, [jnp.max(s, axis=-1, keepdims=True) for s in scores])
    if sink is not None:
        m = jnp.maximum(m, sink)
    ps = [jnp.exp(s - m) for s in scores]
    den = functools.reduce(lambda a, b: a + b, [jnp.sum(p, axis=-1, keepdims=True) for p in ps])
    if sink is not None:
        den = den + jnp.exp(sink - m)
    o = functools.reduce(lambda a, b: a + b,
                         [jnp.dot(p.astype(BF), v, preferred_element_type=F32)
                          for p, v in zip(ps, v_slabs)])
    return o / den


def _attend_heads(q_of, k_of, v_of, bias_of, sink_of, o_ref, *, n_heads, group):
    for qs in range(n_heads // 2):
        q_slab = q_of(qs)
        outs = []
        for q_half in range(2):
            h = 2 * qs + q_half
            kv = h // group
            ks, k_half = kv // 2, kv % 2
            scores = _head_scores(q_slab, q_half, k_half, k_of(ks))
            scores = [s + b for s, b in zip(scores, bias_of(h))]
            o = _softmax_pv(scores, v_of(ks), sink_of(h))
            if q_half != k_half:
                o = pltpu.roll(o, HEAD_DIM, axis=1)
            outs.append(o)
        o = jnp.where(_half_mask(outs[0].shape, 0), outs[0], outs[1])
        o_ref[:, qs * LANES:(qs + 1) * LANES] = o.astype(o_ref.dtype)


def _attn_prompt_kernel(*refs, nkb, n_heads, group, has_sink):
    q_ref = refs[0]
    k_refs = refs[1:1 + nkb]
    v_refs = refs[1 + nkb:1 + 2 * nkb]
    bias_ref = refs[1 + 2 * nkb]
    sink_ref = refs[2 + 2 * nkb] if has_sink else None
    o_ref = refs[-1]
    tq = q_ref.shape[0]
    i = pl.program_id(0)
    scale = HEAD_DIM ** -0.5
    pad = [jnp.where(i - (nkb - 1) + b < 0, NEG_INF, 0.0).astype(F32) for b in range(nkb)]

    def q_of(s):
        return (q_ref[:, s * LANES:(s + 1) * LANES].astype(F32) * scale).astype(BF)

    def k_of(s):
        return [r[:, s * LANES:(s + 1) * LANES] for r in k_refs]

    def v_of(s):
        return [r[:, s * LANES:(s + 1) * LANES] for r in v_refs]

    def bias_of(h):
        hb = h if bias_ref.shape[0] > 1 else 0
        return [bias_ref[hb, :, b * tq:(b + 1) * tq] + pad[b] for b in range(nkb)]

    def sink_of(h):
        return sink_ref[h] if has_sink else None

    _attend_heads(q_of, k_of, v_of, bias_of, sink_of, o_ref, n_heads=n_heads, group=group)


def _attn_prompt(q_arr, q_blk, k_arr, k_blk, v_arr, v_blk, bias, sinks, *, tq, nkb, n_heads, group, name):
    m = q_arr.shape[0]
    qw = n_heads * HEAD_DIM
    kw = (n_heads // group) * HEAD_DIM
    in_specs = [pl.BlockSpec((tq, qw), lambda i: (i, q_blk))]
    args = [q_arr]
    for arr, blk in ((k_arr, k_blk), (v_arr, v_blk)):
        for b in range(nkb):
            in_specs.append(pl.BlockSpec(
                (tq, kw), lambda i, b=b, blk=blk: (jnp.maximum(i - (nkb - 1) + b, 0), blk)))
            args.append(arr)
    in_specs.append(pl.BlockSpec(bias.shape, lambda i: (0, 0, 0)))
    args.append(bias)
    if sinks is not None:
        in_specs.append(pl.BlockSpec(memory_space=pltpu.SMEM))
        args.append(sinks)
    return pl.pallas_call(
        functools.partial(_attn_prompt_kernel, nkb=nkb, n_heads=n_heads, group=group,
                          has_sink=sinks is not None),
        grid=(m // tq,), in_specs=in_specs,
        out_specs=pl.BlockSpec((tq, qw), lambda i: (i, 0)),
        out_shape=jax.ShapeDtypeStruct((m, qw), BF),
        compiler_params=_params("parallel"),
        name=name,
    )(*args)


def _band_bias_tile(table, tq, nkb, n_prev):
    r = np.arange(tq)[:, None]
    s = np.arange(nkb * tq)[None, :]
    q_chunk = (nkb - 1) * (tq // CHUNK) + r // CHUNK
    k_chunk = s // CHUNK
    valid = (k_chunk <= q_chunk) & (k_chunk >= q_chunk - n_prev)
    if table is None:
        return jnp.where(valid, 0.0, NEG_INF).astype(F32)[None]
    dist = (nkb - 1) * tq + r - s
    idx = np.clip(dist, -MAX_REL, MAX_REL) + MAX_REL
    return jnp.where(valid[None], table[:, idx].astype(F32), NEG_INF)


def _attn_sample_kernel(*refs, n_heads, group, has_bias, has_sink):
    q_ref, kn_ref, vn_ref, kc_ref, vc_ref = refs[:5]
    rest = list(refs[5:-1])
    bias_ref = rest.pop(0) if has_bias else None
    sink_ref = rest.pop(0) if has_sink else None
    o_ref = refs[-1]
    rows = kc_ref.shape[0]
    scale = HEAD_DIM ** -0.5

    def q_of(s):
        return (q_ref[:, s * LANES:(s + 1) * LANES].astype(F32) * scale).astype(BF)

    def k_of(s):
        sl = slice(s * LANES, (s + 1) * LANES)
        return [kc_ref[:, sl].astype(BF), kn_ref[:, sl]]

    def v_of(s):
        sl = slice(s * LANES, (s + 1) * LANES)
        return [vc_ref[:, sl].astype(BF), vn_ref[:, sl]]

    def bias_of(h):
        if not has_bias:
            return [0.0, 0.0]
        return [bias_ref[h, :, :rows], bias_ref[h, :, rows:]]

    def sink_of(h):
        return sink_ref[h] if has_sink else None

    _attend_heads(q_of, k_of, v_of, bias_of, sink_of, o_ref, n_heads=n_heads, group=group)


def _attn_sample(q_arr, q_blk, k_arr, k_blk, v_arr, v_blk, k_cache, v_cache, bias, sinks, *,
                 t, n_heads, group, name):
    bsz, rows, kw = k_cache.shape
    qw = n_heads * HEAD_DIM
    in_specs = [pl.BlockSpec((t, qw), lambda i: (i, q_blk)),
                pl.BlockSpec((t, kw), lambda i: (i, k_blk)),
                pl.BlockSpec((t, kw), lambda i: (i, v_blk)),
                pl.BlockSpec((None, rows, kw), lambda i: (i, 0, 0)),
                pl.BlockSpec((None, rows, kw), lambda i: (i, 0, 0))]
    args = [q_arr, k_arr, v_arr, k_cache, v_cache]
    if bias is not None:
        in_specs.append(pl.BlockSpec(bias.shape, lambda i: (0, 0, 0)))
        args.append(bias)
    if sinks is not None:
        in_specs.append(pl.BlockSpec(memory_space=pltpu.SMEM))
        args.append(sinks)
    return pl.pallas_call(
        functools.partial(_attn_sample_kernel, n_heads=n_heads, group=group,
                          has_bias=bias is not None, has_sink=sinks is not None),
        grid=(bsz,), in_specs=in_specs,
        out_specs=pl.BlockSpec((t, qw), lambda i: (i, 0)),
        out_shape=jax.ShapeDtypeStruct((bsz * t, qw), BF),
        compiler_params=_params("parallel"),
        name=name,
    )(*args)


def _merge_kernel(*refs, n_branch):
    br = refs[:n_branch]
    gt = refs[n_branch:2 * n_branch]
    wb = refs[2 * n_branch:3 * n_branch]
    o_ref = refs[-1]
    acc = None
    for b, g, w in zip(br, gt, wb):
        y = g[...].astype(F32) * jnp.dot(b[...], w[...], preferred_element_type=F32)
        acc = y if acc is None else acc + y
    o_ref[...] = acc.astype(o_ref.dtype)


def _merge(branches, gates, w_branch, *, name, bm=512, bn=1024):
    n_branch = len(branches)
    m, kdim = branches[0].shape
    d = w_branch.shape[2]
    bm, bn = min(bm, m), min(bn, d)
    nj = d // bn
    in_specs = [pl.BlockSpec((bm, kdim), lambda j, i: (i, 0)) for _ in range(n_branch)]
    in_specs += [pl.BlockSpec((bm, bn), lambda j, i, b=b: (i, b * nj + j)) for b in range(n_branch)]
    in_specs += [pl.BlockSpec((None, kdim, bn), lambda j, i, b=b: (b, 0, j)) for b in range(n_branch)]
    return pl.pallas_call(
        functools.partial(_merge_kernel, n_branch=n_branch),
        grid=(nj, m // bm), in_specs=in_specs,
        out_specs=pl.BlockSpec((bm, bn), lambda j, i: (i, j)),
        out_shape=jax.ShapeDtypeStruct((m, d), BF),
        compiler_params=_params("parallel", "arbitrary"),
        name=name,
    )(*branches, *([gates] * n_branch), *([w_branch] * n_branch))


def _layer_weights(l, w_in, w_branch, w_out, w_ff1, w_ff2):
    br = w_branch.shape[2]
    d = w_out.shape[1]
    kvw = SWA_KV_HEADS * HEAD_DIM
    o = 0
    w_gmlp = w_in[l, :, o:o + 2 * br]; o += 2 * br
    w_glu = w_in[l, :, o:o + 2 * br]; o += 2 * br
    w_swa = w_in[l, :, o:o + br + 2 * kvw]; o += br + 2 * kvw
    w_band = w_in[l, :, o:o + 3 * br]; o += 3 * br
    w_gate = w_in[l, :, o:]
    run = 512
    w_glu = w_glu.reshape(d, 2, br // run, run).transpose(0, 2, 1, 3).reshape(d, 2 * br)
    cast = lambda w: w.astype(BF)
    return dict(gmlp=cast(w_gmlp), glu=cast(w_glu), swa=cast(w_swa), band=cast(w_band),
                gate=cast(w_gate), branch=cast(w_branch[l]), out=cast(w_out[l]),
                ff1=cast(w_ff1[l]), ff2=cast(w_ff2[l]))


def _run_path(tag, x3, mods, caches, pos, lw_all, p):
    bsz, t, d = x3.shape
    m = bsz * t
    depth = len(lw_all)
    br = d // 2
    n_heads = br // HEAD_DIM
    kvw = SWA_KV_HEADS * HEAD_DIM
    prompt = caches is None
    alpha = (2 * depth) ** 0.25

    def rowwise(a):
        return a if bsz == 1 else jnp.repeat(a, t, axis=0)

    tables = _rope_tables(jnp.tile(pos, bsz))
    blk = min(t, GMLP_BLOCK)
    rep = GMLP_BLOCK // blk

    x = x3.reshape(m, d)
    sh1, sc1 = rowwise(mods[0][0]), rowwise(mods[0][1])
    h = _modulate(x, sc1, sh1, name=f"{tag}_mod0")
    states = []
    for l in range(depth):
        lw = lw_all[l]
        _, _, g1, sh2, sc2, g2 = [rowwise(a) for a in mods[l]]
        ug = _mm(h, lw["gmlp"], act="gelu", name=f"{tag}_proj_gmlp")
        glu = _mm(h, lw["glu"], act="glu", name=f"{tag}_proj_glu")
        swa = _mm(h, lw["swa"], bn=br + 2 * kvw, name=f"{tag}_proj_swa")
        band = _mm(h, lw["band"], name=f"{tag}_proj_band")
        gates = _mm(h, lw["gate"], act="sigmoid", name=f"{tag}_proj_gate")
        ws = jnp.tile(p["w_spatial"][l][:, :blk, :blk], (1, rep, rep))
        bmap = jnp.repeat(jnp.tile(p["b_spatial"][l][:, :blk], (1, rep)).T, br // GMLP_GROUPS, axis=1)
        res = _gmlp(ug, ws, bmap, p["gmlp_ln_g"][l][None], p["gmlp_ln_b"][l][None],
                    blk=blk, emit_vn=not prompt, name=f"{tag}_gmlp")
        out_a = res[0]
        if prompt:
            out_b = _conv(glu, glu, p["conv_w"][l], p["conv_b"][l][None], p["conv_ln_g"][l][None],
                          p["conv_ln_b"][l][None], rows=256, prev_is_glu=True, name=f"{tag}_conv")
        else:
            hist = jnp.pad(caches[0][l], ((0, 0), (CONV_HALO - (CONV_WIDTH - 1), 0), (0, 0)))
            out_b = _conv(glu, hist.reshape(bsz * CONV_HALO, br), p["conv_w"][l], p["conv_b"][l][None],
                          p["conv_ln_g"][l][None], p["conv_ln_b"][l][None], rows=t, prev_is_glu=False,
                          name=f"{tag}_conv")
        new_conv = glu.reshape(bsz, t, br)[:, t - (CONV_WIDTH - 1):].astype(F32)
        qk = _rope(swa, br + kvw, tables, name=f"{tag}_rope")
        sinks = p["swa_sinks"][l]
        if prompt:
            tq_c, nkb_c = 128, 2
            out_c = _attn_prompt(qk, 0, qk, br // kvw, swa, (br + kvw) // kvw,
                                 _band_bias_tile(None, tq_c, nkb_c, SWA_PREV), sinks,
                                 tq=tq_c, nkb=nkb_c, n_heads=n_heads, group=n_heads // SWA_KV_HEADS,
                                 name=f"{tag}_swa")
            tq_d, nkb_d = 128, 5
            out_d = _attn_prompt(band, 0, band, 1, band, 2,
                                 _band_bias_tile(p["band_rel_bias"][l], tq_d, nkb_d, BAND_PREV), None,
                                 tq=tq_d, nkb=nkb_d, n_heads=n_heads, group=1, name=f"{tag}_band")
            swa_k = qk[m - SWA_PREV * CHUNK:, br:].astype(F32).reshape(bsz, -1, SWA_KV_HEADS, HEAD_DIM)
            swa_v = swa[m - SWA_PREV * CHUNK:, br + kvw:].astype(F32).reshape(bsz, -1, SWA_KV_HEADS, HEAD_DIM)
            band_k = band[m - BAND_PREV * CHUNK:, br:2 * br].astype(F32).reshape(bsz, -1, n_heads, HEAD_DIM)
            band_v = band[m - BAND_PREV * CHUNK:, 2 * br:].astype(F32).reshape(bsz, -1, n_heads, HEAD_DIM)
        else:
            ck, cv, bk, bv = (c[l] for c in caches[1:])
            rows_c, rows_d = ck.shape[1], bk.shape[1]
            out_c = _attn_sample(qk, 0, qk, br // kvw, swa, (br + kvw) // kvw,
                                 ck.reshape(bsz, rows_c, kvw), cv.reshape(bsz, rows_c, kvw), None, sinks,
                                 t=t, n_heads=n_heads, group=n_heads // SWA_KV_HEADS, name=f"{tag}_swa")
            dist = rows_d + jnp.arange(t)[:, None] - jnp.arange(rows_d + t)[None, :]
            bias = p["band_rel_bias"][l][:, jnp.clip(dist, -MAX_REL, MAX_REL) + MAX_REL].astype(F32)
            out_d = _attn_sample(band, 0, band, 1, band, 2,
                                 bk.reshape(bsz, rows_d, br), bv.reshape(bsz, rows_d, br), bias, None,
                                 t=t, n_heads=n_heads, group=1, name=f"{tag}_band")
            new = lambda a, heads: a.astype(F32).reshape(bsz, t, heads, HEAD_DIM)
            keep = lambda c, n: jnp.concatenate([c, n], axis=1)[:, -c.shape[1]:]
            swa_k = keep(ck, new(qk[:, br:], SWA_KV_HEADS))
            swa_v = keep(cv, new(swa[:, br + kvw:], SWA_KV_HEADS))
            band_k = keep(bk, new(band[:, br:2 * br], n_heads))
            band_v = keep(bv, new(band[:, 2 * br:], n_heads))
        merged = _merge([out_a, out_b, out_c, out_d], gates, lw["branch"], name=f"{tag}_merge")
        x, h2 = _mm_ln(merged, lw["out"], x, g1, p["ln1_g"][l][None], p["ln1_b"][l][None], sc2, sh2,
                       alpha=alpha, name=f"{tag}_out_ln")
        hid = _mm(h2, lw["ff1"], act="relu2", name=f"{tag}_ff1")
        if l + 1 < depth:
            nsh, nsc = rowwise(mods[l + 1][0]), rowwise(mods[l + 1][1])
        else:
            nsh, nsc = sh2, sc2
        x, h = _mm_ln(hid, lw["ff2"], x, g2, p["ln2_g"][l][None], p["ln2_b"][l][None], nsc, nsh,
                      alpha=alpha, name=f"{tag}_ff2_ln")
        st = [new_conv, swa_k, swa_v, band_k, band_v]
        if not prompt:
            st.append(res[1].reshape(bsz, t, br))
        states.append(st)
    stacked = [jnp.stack(s, axis=0) for s in zip(*states)]
    return x.reshape(bsz, t, d), stacked


def kernel(x_prompt, x_sample, state_conv, cache_swa_k, cache_swa_v, cache_band_k, cache_band_v, c_prompt, c_sample, w_ada, b_ada, w_in, gmlp_ln_g, gmlp_ln_b, w_spatial, b_spatial, conv_w, conv_b, conv_ln_g, conv_ln_b, swa_sinks, band_rel_bias, w_branch, w_out, ln1_g, ln1_b, w_ff1, w_ff2, ln2_g, ln2_b):
    depth = w_in.shape[0]
    d = x_prompt.shape[2]
    p = dict(gmlp_ln_g=gmlp_ln_g, gmlp_ln_b=gmlp_ln_b, w_spatial=w_spatial, b_spatial=b_spatial,
             conv_w=conv_w, conv_b=conv_b, conv_ln_g=conv_ln_g, conv_ln_b=conv_ln_b,
             swa_sinks=swa_sinks, band_rel_bias=band_rel_bias, ln1_g=ln1_g, ln1_b=ln1_b,
             ln2_g=ln2_g, ln2_b=ln2_b)
    nb_p, nb_s = c_prompt.shape[0], c_sample.shape[0]
    pad = (-(nb_p + nb_s)) % 8
    c_all = jnp.concatenate([c_prompt, c_sample, jnp.zeros((pad, d), F32)], axis=0)
    mods_p, mods_s = [], []
    for l in range(depth):
        mod = _mm(c_all, w_ada[l].astype(BF), pre="silu", bias=b_ada[l][None], out_dtype=F32,
                  name="adaln")
        parts = jnp.split(mod, 6, axis=-1)
        mods_p.append([a[:nb_p] for a in parts])
        mods_s.append([a[nb_p:nb_p + nb_s] for a in parts])
    lw_all = [_layer_weights(l, w_in, w_branch, w_out, w_ff1, w_ff2) for l in range(depth)]
    pos_p = jnp.arange(x_prompt.shape[1], dtype=jnp.int32)
    pos_s = PAST_LEN + jnp.arange(x_sample.shape[1], dtype=jnp.int32)
    y_p, st_p = _run_path("p", x_prompt, mods_p, None, pos_p, lw_all, p)
    y_s, st_s = _run_path("s", x_sample, mods_s,
                          (state_conv, cache_swa_k, cache_swa_v, cache_band_k, cache_band_v),
                          pos_s, lw_all, p)
    conv_p, swa_k_p, swa_v_p, band_k_p, band_v_p = st_p
    conv_s, swa_k_s, swa_v_s, band_k_s, band_v_s, gmlp_v_s = st_s
    return (y_p, y_s, conv_p, conv_s, swa_k_p, swa_v_p, swa_k_s, swa_v_s,
            band_k_p, band_v_p, band_k_s, band_v_s, gmlp_v_s)
```

```python
import functools

import jax
import jax.numpy as jnp
import numpy as np
from jax import lax
from jax.experimental import pallas as pl
from jax.experimental.pallas import tpu as pltpu

BF = jnp.bfloat16
F32 = jnp.float32

CHUNK = 64
HEAD_DIM = 64
GMLP_BLOCK = 128
GMLP_GROUPS = 8
CONV_WIDTH = 31
SWA_KV_HEADS = 4
SWA_PREV = 2
BAND_PREV = 8
MAX_REL = 256
PAST_LEN = 1024
ROPE_THETA = 500000.0
ROT_DIM = HEAD_DIM // 4
LN_EPS = 1e-5
NEG_INF = -1e30

LANES = 128
SUBLANES = 8
CONV_HALO = 32
VMEM_LIMIT = 56 * 1024 * 1024


def _params(*sem):
    return pltpu.CompilerParams(dimension_semantics=sem, vmem_limit_bytes=VMEM_LIMIT)


def _ln(x, g, b):
    mu = jnp.mean(x, axis=-1, keepdims=True)
    xc = x - mu
    var = jnp.mean(xc * xc, axis=-1, keepdims=True)
    return xc * lax.rsqrt(var + LN_EPS) * g + b


def _sigmoid(x):
    return jax.nn.sigmoid(x)


def _mm_kernel(*refs, act, pre, has_bias):
    a_ref, o_ref = refs[0], refs[-1]
    a = a_ref[...]
    if pre == "silu":
        a = a * _sigmoid(a)
    a = a.astype(BF)
    acc = jnp.dot(a, refs[1][...], preferred_element_type=F32)
    if has_bias:
        acc = acc + refs[2][...]
    if act == "gelu":
        acc = jax.nn.gelu(acc)
    elif act == "sigmoid":
        acc = _sigmoid(acc)
    elif act == "relu2":
        acc = jnp.square(jnp.maximum(acc, 0.0))
    elif act == "glu":
        acc = acc * _sigmoid(jnp.dot(a, refs[2][...], preferred_element_type=F32))
    o_ref[...] = acc.astype(o_ref.dtype)


def _mm(a, w, l, col0, n, *, name, act=None, pre=None, bias=None, out_dtype=BF, bm=1024, bn=1024):
    m, k = a.shape
    bm, bn = min(bm, m), min(bn, n)
    assert m % bm == 0 and n % bn == 0 and col0 % bn == 0, (m, n, bm, bn, col0)
    c0 = col0 // bn
    in_specs = [pl.BlockSpec((bm, k), lambda i, j: (i, 0)),
                pl.BlockSpec((None, k, bn), lambda i, j: (l, 0, c0 + j))]
    args = [a, w]
    if act == "glu":
        g0 = (col0 + n) // bn
        in_specs.append(pl.BlockSpec((None, k, bn), lambda i, j: (l, 0, g0 + j)))
        args.append(w)
    if bias is not None:
        in_specs.append(pl.BlockSpec((None, 1, bn), lambda i, j: (l, 0, j)))
        args.append(bias)
    return pl.pallas_call(
        functools.partial(_mm_kernel, act=act, pre=pre, has_bias=bias is not None),
        grid=(m // bm, n // bn),
        in_specs=in_specs,
        out_specs=pl.BlockSpec((bm, bn), lambda i, j: (i, j)),
        out_shape=jax.ShapeDtypeStruct((m, n), out_dtype),
        compiler_params=_params("parallel", "arbitrary"),
        name=name,
    )(*args)


LN_ROWS = 128


def _mm_ln_kernel(a_ref, w_ref, x_ref, g_ref, lng_ref, lnb_ref, sc_ref, sh_ref,
                  xo_ref, ho_ref, *, alpha, nk):
    k = pl.program_id(1)

    def rows_of(ref, rows):
        return ref[...] if ref.shape[0] == 1 else ref[rows, :]

    def finish(accumulated):
        for r in range(xo_ref.shape[0] // LN_ROWS):
            rows = slice(r * LN_ROWS, (r + 1) * LN_ROWS)
            acc = jnp.dot(a_ref[rows, :], w_ref[...], preferred_element_type=F32)
            if accumulated:
                acc = acc + xo_ref[rows, :]
            y = alpha * x_ref[rows, :] + (1.0 + rows_of(g_ref, rows)) * acc
            xn = _ln(y, lng_ref[...], lnb_ref[...])
            xo_ref[rows, :] = xn
            ho_ref[rows, :] = (xn * (1.0 + rows_of(sc_ref, rows)) + rows_of(sh_ref, rows)).astype(ho_ref.dtype)

    if nk == 1:
        finish(False)
        return

    @pl.when(k == 0)
    def _():
        xo_ref[...] = jnp.dot(a_ref[...], w_ref[...], preferred_element_type=F32)

    @pl.when((k > 0) & (k < nk - 1))
    def _():
        xo_ref[...] += jnp.dot(a_ref[...], w_ref[...], preferred_element_type=F32)

    @pl.when(k == nk - 1)
    def _():
        finish(True)


def _row_spec(arr, bm, d):
    if arr.shape[0] == 1:
        return pl.BlockSpec((1, d), lambda i, k: (0, 0))
    return pl.BlockSpec((bm, d), lambda i, k: (i, 0))


def _mm_ln(a, w, l, x, gate, ln_g, ln_b, scale, shift, *, alpha, name, bm=512, bk=2048):
    m, kdim = a.shape
    d = w.shape[2]
    if gate.shape[0] != 1:
        bm //= 2
    bm, bk = min(bm, m), min(bk, kdim)
    assert m % bm == 0 and kdim % bk == 0 and bm % LN_ROWS == 0
    nk = kdim // bk
    const = pl.BlockSpec((None, 1, d), lambda i, k: (l, 0, 0))
    rows = pl.BlockSpec((bm, d), lambda i, k: (i, 0))
    return pl.pallas_call(
        functools.partial(_mm_ln_kernel, alpha=alpha, nk=nk),
        grid=(m // bm, nk),
        in_specs=[pl.BlockSpec((bm, bk), lambda i, k: (i, k)),
                  pl.BlockSpec((None, bk, d), lambda i, k: (l, k, 0)),
                  rows, _row_spec(gate, bm, d), const, const,
                  _row_spec(scale, bm, d), _row_spec(shift, bm, d)],
        out_specs=[rows, rows],
        out_shape=[jax.ShapeDtypeStruct((m, d), F32), jax.ShapeDtypeStruct((m, d), BF)],
        compiler_params=_params("parallel", "arbitrary"),
        name=name,
    )(a, w, x, gate, ln_g, ln_b, scale, shift)


def _mod_kernel(x_ref, sc_ref, sh_ref, o_ref):
    o_ref[...] = (x_ref[...] * (1.0 + sc_ref[...]) + sh_ref[...]).astype(o_ref.dtype)


def _modulate(x, scale, shift, *, name, bm=512):
    m, d = x.shape
    bm = min(bm, m)
    rows = pl.BlockSpec((bm, d), lambda i: (i, 0))

    def spec(arr):
        if arr.shape[0] == 1:
            return pl.BlockSpec((1, d), lambda i: (0, 0))
        return rows

    return pl.pallas_call(
        _mod_kernel, grid=(m // bm,),
        in_specs=[rows, spec(scale), spec(shift)],
        out_specs=rows,
        out_shape=jax.ShapeDtypeStruct((m, d), BF),
        compiler_params=_params("parallel"),
        name=name,
    )(x, scale, shift)


def _rope_kernel(x_ref, c_ref, sa_ref, sb_ref, o_ref):
    half = ROT_DIM // 2
    c, sa, sb = c_ref[...], sa_ref[...], sb_ref[...]
    for s in range(x_ref.shape[1] // LANES):
        x = x_ref[:, s * LANES:(s + 1) * LANES].astype(F32)
        up = pltpu.roll(x, LANES - half, axis=1)
        dn = pltpu.roll(x, half, axis=1)
        o_ref[:, s * LANES:(s + 1) * LANES] = (x * c + up * sa + dn * sb).astype(o_ref.dtype)


def _rope(slab, width, tables, *, name, bm=512):
    m = slab.shape[0]
    bm = min(bm, m)
    tab = pl.BlockSpec((bm, LANES), lambda i: (i, 0))
    return pl.pallas_call(
        _rope_kernel, grid=(m // bm,),
        in_specs=[pl.BlockSpec((bm, width), lambda i: (i, 0)), tab, tab, tab],
        out_specs=pl.BlockSpec((bm, width), lambda i: (i, 0)),
        out_shape=jax.ShapeDtypeStruct((m, width), BF),
        compiler_params=_params("parallel"),
        name=name,
    )(slab, *tables)


def _rope_tables(pos):
    half = ROT_DIM // 2
    inv_freq = jnp.power(jnp.float32(ROPE_THETA), -jnp.arange(half, dtype=F32) / half)
    ang = pos.astype(F32)[:, None] * inv_freq[None, :]
    cos, sin = jnp.cos(ang), jnp.sin(ang)
    t = pos.shape[0]
    ones = jnp.ones((t, HEAD_DIM - ROT_DIM), F32)
    zeros = jnp.zeros((t, HEAD_DIM - ROT_DIM), F32)
    zh = jnp.zeros((t, half), F32)
    c = jnp.concatenate([cos, cos, ones], axis=1)
    sa = jnp.concatenate([-sin, zh, zeros], axis=1)
    sb = jnp.concatenate([zh, sin, zeros], axis=1)
    rep = LANES // HEAD_DIM
    return tuple(jnp.tile(a, (1, rep)) for a in (c, sa, sb))


def _gmlp_kernel(ug_ref, ws_ref, bmap_ref, lng_ref, lnb_ref, o_ref, *vn_refs, blk):
    width = o_ref.shape[1]
    gc = width // GMLP_GROUPS
    t = lax.broadcasted_iota(jnp.int32, (GMLP_BLOCK, GMLP_BLOCK), 0)
    s = lax.broadcasted_iota(jnp.int32, (GMLP_BLOCK, GMLP_BLOCK), 1)
    mask = (s <= t) & (s >= (t // blk) * blk)
    ws = [jnp.where(mask, ws_ref[g], 0.0).astype(BF) for g in range(GMLP_GROUPS)]
    bmap = bmap_ref[...]
    for r in range(o_ref.shape[0] // GMLP_BLOCK):
        rows = slice(r * GMLP_BLOCK, (r + 1) * GMLP_BLOCK)
        vn = _ln(ug_ref[rows, width:].astype(F32), lng_ref[...], lnb_ref[...])
        if vn_refs:
            vn_refs[0][rows, :] = vn
        vnb = vn.astype(BF)
        for g in range(GMLP_GROUPS):
            cols = slice(g * gc, (g + 1) * gc)
            mixed = jnp.dot(ws[g], vnb[:, cols], preferred_element_type=F32)
            u = ug_ref[rows, cols].astype(F32)
            o_ref[rows, cols] = (u * (mixed + bmap[:, cols])).astype(o_ref.dtype)


def _gmlp(ug, ws, bmap, ln_g, ln_b, l, *, blk, emit_vn, name, bm=512):
    m, two_w = ug.shape
    width = two_w // 2
    bm = min(bm, m)
    vec = pl.BlockSpec((None, 1, width), lambda i: (l, 0, 0))
    out_specs = [pl.BlockSpec((bm, width), lambda i: (i, 0))]
    out_shape = [jax.ShapeDtypeStruct((m, width), BF)]
    if emit_vn:
        out_specs.append(pl.BlockSpec((bm, width), lambda i: (i, 0)))
        out_shape.append(jax.ShapeDtypeStruct((m, width), F32))
    return pl.pallas_call(
        functools.partial(_gmlp_kernel, blk=blk), grid=(m // bm,),
        in_specs=[pl.BlockSpec((bm, two_w), lambda i: (i, 0)),
                  pl.BlockSpec((None, GMLP_GROUPS, GMLP_BLOCK, GMLP_BLOCK), lambda i: (l, 0, 0, 0)),
                  pl.BlockSpec((None, GMLP_BLOCK, width), lambda i: (l, 0, 0)),
                  vec, vec],
        out_specs=out_specs, out_shape=out_shape,
        compiler_params=_params("parallel"),
        name=name,
    )(ug, ws, bmap, ln_g, ln_b)


CONV_ROWS = 16
CONV_COPY_ROWS = 56


def _conv_kernel(cur_ref, prev_ref, w_ref, cb_ref, lng_ref, lnb_ref, o_ref, xs_ref, *, zero_first):
    rows = cur_ref.shape[0]
    prev = prev_ref[...].astype(F32)
    if zero_first:
        prev = jnp.where(pl.program_id(0) == 0, 0.0, prev)
    xs_ref[0, 0:CONV_HALO, :] = prev
    xs_ref[0, CONV_HALO:CONV_HALO + rows, :] = cur_ref[...].astype(F32)
    n_shift = CONV_HALO + rows - SUBLANES
    for b in range(1, SUBLANES):
        for r0 in range(0, n_shift, CONV_COPY_ROWS):
            xs_ref[b, r0:r0 + CONV_COPY_ROWS, :] = xs_ref[0, r0 + b:r0 + b + CONV_COPY_ROWS, :]
    first = CONV_HALO - (CONV_WIDTH - 1)
    for c in range(rows // CONV_ROWS):
        acc = jnp.zeros((CONV_ROWS, cur_ref.shape[1]), F32)
        for j in range(CONV_WIDTH):
            b = (first + j) % SUBLANES
            r0 = c * CONV_ROWS + first + j - b
            acc = acc + w_ref[j:j + 1, :] * xs_ref[b, r0:r0 + CONV_ROWS, :]
        y = _ln(acc + cb_ref[...], lng_ref[...], lnb_ref[...])
        o_ref[c * CONV_ROWS:(c + 1) * CONV_ROWS, :] = (y * _sigmoid(y)).astype(o_ref.dtype)


def _conv(glu, prev, conv_w, conv_b, ln_g, ln_b, l, *, rows, prev_is_glu, name):
    m, width = glu.shape
    rows = min(rows, m)
    step = rows // CONV_HALO
    if prev_is_glu:
        prev_spec = pl.BlockSpec((CONV_HALO, width), lambda i: (jnp.maximum(i * step - 1, 0), 0))
    else:
        prev_spec = pl.BlockSpec((CONV_HALO, width), lambda i: (i, 0))
    vec = pl.BlockSpec((None, 1, width), lambda i: (l, 0, 0))
    return pl.pallas_call(
        functools.partial(_conv_kernel, zero_first=prev_is_glu), grid=(m // rows,),
        in_specs=[pl.BlockSpec((rows, width), lambda i: (i, 0)), prev_spec,
                  pl.BlockSpec((None, CONV_WIDTH, width), lambda i: (l, 0, 0)), vec, vec, vec],
        out_specs=pl.BlockSpec((rows, width), lambda i: (i, 0)),
        out_shape=jax.ShapeDtypeStruct((m, width), BF),
        scratch_shapes=[pltpu.VMEM((SUBLANES, CONV_HALO + rows, width), F32)],
        compiler_params=_params("parallel"),
        name=name,
    )(glu, prev, conv_w, conv_b, ln_g, ln_b)


def _half_mask(shape, half):
    lane = lax.broadcasted_iota(jnp.int32, shape, len(shape) - 1)
    return (lane >= HEAD_DIM) if half else (lane < HEAD_DIM)


def _stack_queries(q_ref, h0, per_slab, group):
    scale = HEAD_DIM ** -0.5
    blocks = []
    for qs in range(h0 // 2, (h0 + per_slab) // 2):
        q = (q_ref[:, qs * LANES:(qs + 1) * LANES].astype(F32) * scale).astype(BF)
        for q_half in range(2):
            k_half = ((2 * qs + q_half) // group) % 2
            qh = q if q_half == k_half else pltpu.roll(q, HEAD_DIM, axis=1)
            blocks.append(jnp.where(_half_mask(qh.shape, k_half), qh, jnp.zeros_like(qh)))
    return jnp.concatenate(blocks, axis=0)


def _attend_cols(q_ref, k_of, v_of, bias_of, sink_of, o_ref, *, n_heads, group):
    tq = q_ref.shape[0]
    per_slab = 2 * group
    n_slabs = n_heads // per_slab

    def scores(ks):
        qst = _stack_queries(q_ref, ks * per_slab, per_slab, group)
        s = lax.dot_general(k_of(ks), qst, (((1,), (1,)), ((), ())), preferred_element_type=F32)
        return bias_of(ks, s)

    s_next = scores(0)
    for ks in range(n_slabs):
        h0 = ks * per_slab
        s = s_next
        if ks + 1 < n_slabs:
            s_next = scores(ks + 1)
        m = jnp.max(s, axis=0, keepdims=True)
        if sink_of is not None:
            sink = jnp.concatenate([jnp.full((1, tq), sink_of(h), F32)
                                    for h in range(h0, h0 + per_slab)], axis=1)
            m = jnp.maximum(m, sink)
        p = jnp.exp(s - m)
        den = jnp.sum(p, axis=0, keepdims=True)
        if sink_of is not None:
            den = den + jnp.exp(sink - m)
        ot = lax.dot_general(v_of(ks), p.astype(BF), (((0,), (0,)), ((), ())),
                             preferred_element_type=F32)
        ot = ot / den
        for qs in range(h0 // 2, (h0 + per_slab) // 2):
            parts = []
            for q_half in range(2):
                h = 2 * qs + q_half
                k_half = (h // group) % 2
                parts.append(ot[k_half * HEAD_DIM:(k_half + 1) * HEAD_DIM, (h - h0) * tq:(h - h0 + 1) * tq])
            o_ref[:, qs * LANES:(qs + 1) * LANES] = jnp.concatenate(parts, axis=0).T.astype(o_ref.dtype)


def _attend(q_ref, k_of, v_of, bias_of, sink_of, o_ref, *, n_heads, group):
    tq = q_ref.shape[0]
    per_slab = 2 * group
    for ks in range(n_heads // per_slab):
        h0 = ks * per_slab
        qst = _stack_queries(q_ref, h0, per_slab, group)
        scores = [lax.dot_general(qst, kp, (((1,), (1,)), ((), ())), preferred_element_type=F32)
                  for kp in k_of(ks)]
        scores = [s.reshape(per_slab, tq, s.shape[1]) + b
                  for s, b in zip(scores, bias_of(h0, per_slab))]
        m = functools.reduce(jnp.maximum, [jnp.max(s, axis=-1, keepdims=True) for s in scores])
        if sink_of is not None:
            sink = jnp.concatenate([jnp.full((1, tq, 1), sink_of(h), F32)
                                    for h in range(h0, h0 + per_slab)], axis=0)
            m = jnp.maximum(m, sink)
        ps = [jnp.exp(s - m) for s in scores]
        den = functools.reduce(lambda a, b: a + b, [jnp.sum(p, axis=-1, keepdims=True) for p in ps])
        if sink_of is not None:
            den = den + jnp.exp(sink - m)
        o = functools.reduce(
            lambda a, b: a + b,
            [jnp.dot(p.reshape(per_slab * tq, p.shape[2]).astype(BF), vp, preferred_element_type=F32)
             for p, vp in zip(ps, v_of(ks))])
        o = o.reshape(per_slab, tq, LANES) / den
        for qs in range(h0 // 2, (h0 + per_slab) // 2):
            halves = []
            for q_half in range(2):
                h = 2 * qs + q_half
                oh = o[h - h0]
                if (h // group) % 2 != q_half:
                    oh = pltpu.roll(oh, HEAD_DIM, axis=1)
                halves.append(oh)
            out = jnp.where(_half_mask(halves[0].shape, 0), halves[0], halves[1])
            o_ref[:, qs * LANES:(qs + 1) * LANES] = out.astype(o_ref.dtype)


def _attn_prompt_kernel(*refs, nkb, n_heads, group, has_sink, layer):
    q_ref = refs[0]
    k_refs = refs[1:1 + nkb]
    v_refs = refs[1 + nkb:1 + 2 * nkb]
    bias_ref = refs[1 + 2 * nkb]
    sink_ref = refs[2 + 2 * nkb] if has_sink else None
    o_ref = refs[-1]
    tq = q_ref.shape[0]
    first_key = (nkb - 1 - pl.program_id(0)) * tq

    def k_of(s):
        return jnp.concatenate([r[:, s * LANES:(s + 1) * LANES] for r in k_refs], axis=0)

    def v_of(s):
        return jnp.concatenate([r[:, s * LANES:(s + 1) * LANES] for r in v_refs], axis=0)

    def bias_of(ks, s):
        b = bias_ref[ks] if bias_ref.shape[0] > 1 else bias_ref[0]
        key = lax.broadcasted_iota(jnp.int32, s.shape, 0)
        return jnp.where(key < first_key, NEG_INF, s + b)

    sink_of = (lambda h: sink_ref[layer, h]) if has_sink else None
    _attend_cols(q_ref, k_of, v_of, bias_of, sink_of, o_ref, n_heads=n_heads, group=group)


def _attn_prompt(q_arr, q_blk, k_arr, k_blk, v_arr, v_blk, bias, sinks, l, *, tq, nkb, n_heads, group, name):
    m = q_arr.shape[0]
    qw = n_heads * HEAD_DIM
    kw = (n_heads // group) * HEAD_DIM
    in_specs = [pl.BlockSpec((tq, qw), lambda i: (i, q_blk))]
    args = [q_arr]
    for arr, blk in ((k_arr, k_blk), (v_arr, v_blk)):
        for b in range(nkb):
            in_specs.append(pl.BlockSpec(
                (tq, kw), lambda i, b=b, blk=blk: (jnp.maximum(i - (nkb - 1) + b, 0), blk)))
            args.append(arr)
    in_specs.append(pl.BlockSpec((None,) + bias.shape[1:], lambda i: (l, 0, 0, 0)))
    args.append(bias)
    if sinks is not None:
        in_specs.append(pl.BlockSpec(memory_space=pltpu.SMEM))
        args.append(sinks)
    return pl.pallas_call(
        functools.partial(_attn_prompt_kernel, nkb=nkb, n_heads=n_heads, group=group,
                          has_sink=sinks is not None, layer=l),
        grid=(m // tq,), in_specs=in_specs,
        out_specs=pl.BlockSpec((tq, qw), lambda i: (i, 0)),
        out_shape=jax.ShapeDtypeStruct((m, qw), BF),
        compiler_params=_params("parallel"),
        name=name,
    )(*args)


def _rel_bias(table, n_q, n_k, offset):
    d_min, d_max = offset - (n_k - 1), offset + n_q - 1
    lo, hi = max(d_min, -MAX_REL), min(d_max, MAX_REL)
    lead = table.shape[:-1]
    ext = jnp.concatenate(
        [jnp.broadcast_to(table[..., :1], lead + (lo - d_min + 1,)),
         table[..., lo + MAX_REL:hi + MAX_REL + 1],
         jnp.broadcast_to(table[..., -1:], lead + (d_max - hi,))], axis=-1)
    length = n_q + n_k
    rev = ext[..., ::-1]
    flat = jnp.tile(rev, (1,) * len(lead) + (n_q,))[..., :n_q * (length - 1)]
    toep = flat.reshape(lead + (n_q, length - 1))
    return toep[..., n_q - 1:n_q - 1 + n_k]


def _band_mask(tq, nkb, n_prev):
    r = np.arange(tq)[:, None]
    s = np.arange(nkb * tq)[None, :]
    q_chunk = (nkb - 1) * (tq // CHUNK) + r // CHUNK
    k_chunk = s // CHUNK
    return (k_chunk <= q_chunk) & (k_chunk >= q_chunk - n_prev)


def _attn_sample_kernel(*refs, n_heads, group, has_bias, has_sink, layer):
    q_ref, kn_ref, vn_ref, kc_ref, vc_ref = refs[:5]
    rest = list(refs[5:-1])
    bias_ref = rest.pop(0) if has_bias else None
    sink_ref = rest.pop(0) if has_sink else None
    o_ref = refs[-1]
    rows = kc_ref.shape[0]

    def k_of(s):
        sl = slice(s * LANES, (s + 1) * LANES)
        return [kc_ref[:, sl].astype(BF), kn_ref[:, sl]]

    def v_of(s):
        sl = slice(s * LANES, (s + 1) * LANES)
        return [vc_ref[:, sl].astype(BF), vn_ref[:, sl]]

    def bias_of(h0, nh):
        if not has_bias:
            return [0.0, 0.0]
        return [bias_ref[h0:h0 + nh, :, :rows], bias_ref[h0:h0 + nh, :, rows:]]

    sink_of = (lambda h: sink_ref[layer, h]) if has_sink else None
    _attend(q_ref, k_of, v_of, bias_of, sink_of, o_ref, n_heads=n_heads, group=group)


def _attn_sample(q_arr, q_blk, k_arr, k_blk, v_arr, v_blk, k_cache, v_cache, bias, sinks, l, *,
                 t, n_heads, group, name):
    _, bsz, rows, kw = k_cache.shape
    qw = n_heads * HEAD_DIM
    in_specs = [pl.BlockSpec((t, qw), lambda i: (i, q_blk)),
                pl.BlockSpec((t, kw), lambda i: (i, k_blk)),
                pl.BlockSpec((t, kw), lambda i: (i, v_blk)),
                pl.BlockSpec((None, None, rows, kw), lambda i: (l, i, 0, 0)),
                pl.BlockSpec((None, None, rows, kw), lambda i: (l, i, 0, 0))]
    args = [q_arr, k_arr, v_arr, k_cache, v_cache]
    if bias is not None:
        in_specs.append(pl.BlockSpec((None,) + bias.shape[1:], lambda i: (l, 0, 0, 0)))
        args.append(bias)
    if sinks is not None:
        in_specs.append(pl.BlockSpec(memory_space=pltpu.SMEM))
        args.append(sinks)
    return pl.pallas_call(
        functools.partial(_attn_sample_kernel, n_heads=n_heads, group=group,
                          has_bias=bias is not None, has_sink=sinks is not None, layer=l),
        grid=(bsz,), in_specs=in_specs,
        out_specs=pl.BlockSpec((t, qw), lambda i: (i, 0)),
        out_shape=jax.ShapeDtypeStruct((bsz * t, qw), BF),
        compiler_params=_params("parallel"),
        name=name,
    )(*args)


def _merge_kernel(*refs, n_branch):
    br = refs[:n_branch]
    gt = refs[n_branch:2 * n_branch]
    wb = refs[2 * n_branch:3 * n_branch]
    o_ref = refs[-1]
    acc = None
    for b, g, w in zip(br, gt, wb):
        y = g[...].astype(F32) * jnp.dot(b[...], w[...], preferred_element_type=F32)
        acc = y if acc is None else acc + y
    o_ref[...] = acc.astype(o_ref.dtype)


def _merge(branches, gates, w_branch, l, *, name, bm=512, bn=1024):
    n_branch = len(branches)
    m, kdim = branches[0].shape
    d = w_branch.shape[3]
    bm, bn = min(bm, m), min(bn, d)
    nj = d // bn
    in_specs = [pl.BlockSpec((bm, kdim), lambda j, i: (i, 0)) for _ in range(n_branch)]
    in_specs += [pl.BlockSpec((bm, bn), lambda j, i, b=b: (i, b * nj + j)) for b in range(n_branch)]
    in_specs += [pl.BlockSpec((None, None, kdim, bn), lambda j, i, b=b: (l, b, 0, j)) for b in range(n_branch)]
    return pl.pallas_call(
        functools.partial(_merge_kernel, n_branch=n_branch),
        grid=(nj, m // bm), in_specs=in_specs,
        out_specs=pl.BlockSpec((bm, bn), lambda j, i: (i, j)),
        out_shape=jax.ShapeDtypeStruct((m, d), BF),
        compiler_params=_params("parallel", "arbitrary"),
        name=name,
    )(*branches, *([gates] * n_branch), *([w_branch] * n_branch))


def _run_path(tag, x3, mods, caches, pos, w, p):
    bsz, t, d = x3.shape
    m = bsz * t
    depth = w["in"].shape[0]
    br = d // 2
    n_heads = br // HEAD_DIM
    kvw = SWA_KV_HEADS * HEAD_DIM
    prompt = caches is None
    alpha = (2 * depth) ** 0.25
    col = w["cols"]

    def rowwise(a):
        return a if bsz == 1 else jnp.repeat(a, t, axis=0)

    def vec(a):
        return a[:, None, :]

    tables = _rope_tables(jnp.tile(pos, bsz))
    blk = min(t, GMLP_BLOCK)
    rep = GMLP_BLOCK // blk
    ws = jnp.tile(p["w_spatial"][:, :, :blk, :blk], (1, 1, rep, rep))
    bmap = jnp.repeat(jnp.swapaxes(jnp.tile(p["b_spatial"][:, :, :blk], (1, 1, rep)), 1, 2),
                      br // GMLP_GROUPS, axis=2)
    sinks = p["swa_sinks"]
    if prompt:
        tq_c, nkb_c, tq_d, nkb_d = 128, 2, 128, 5
        per_c = 2 * n_heads // SWA_KV_HEADS
        bias_c = jnp.where(_band_mask(tq_c, nkb_c, SWA_PREV), 0.0, NEG_INF).astype(F32).T
        bias_c = jnp.broadcast_to(jnp.tile(bias_c, (1, per_c)), (depth, 1, nkb_c * tq_c, per_c * tq_c))
        bias_d = jnp.where(_band_mask(tq_d, nkb_d, BAND_PREV),
                           _rel_bias(p["band_rel_bias"], tq_d, nkb_d * tq_d, (nkb_d - 1) * tq_d).astype(F32),
                           NEG_INF)
        bias_d = bias_d.reshape(depth, n_heads // 2, 2 * tq_d, nkb_d * tq_d).swapaxes(2, 3)
    else:
        rows_d = caches[3].shape[2]
        bias_d = _rel_bias(p["band_rel_bias"], t, rows_d + t, rows_d).astype(F32)
        hist = jnp.pad(caches[0], ((0, 0), (0, 0), (CONV_HALO - (CONV_WIDTH - 1), 0), (0, 0)))
        hist = hist.reshape(depth, bsz * CONV_HALO, br)
        flat = lambda c: c.reshape(c.shape[:3] + (-1,))
        ck, cv, bk, bv = (flat(c) for c in caches[1:])

    x = x3.reshape(m, d)
    h = _modulate(x, rowwise(mods[0][1]), rowwise(mods[0][0]), name=f"{tag}_mod0")
    states = []
    for l in range(depth):
        _, _, g1, sh2, sc2, g2 = [rowwise(a) for a in mods[l]]
        ug = _mm(h, w["in"], l, col["gmlp"], 2 * br, act="gelu", name=f"{tag}_proj_gmlp")
        glu = _mm(h, w["in"], l, col["glu"], br, act="glu", bn=512, name=f"{tag}_proj_glu")
        swa = _mm(h, w["in"], l, col["swa"], br + 2 * kvw, bn=br + 2 * kvw, name=f"{tag}_proj_swa")
        band = _mm(h, w["in"], l, col["band"], 3 * br, name=f"{tag}_proj_band")
        gates = _mm(h, w["in"], l, col["gate"], 4 * d, act="sigmoid", name=f"{tag}_proj_gate")
        res = _gmlp(ug, ws, bmap, vec(p["gmlp_ln_g"]), vec(p["gmlp_ln_b"]), l,
                    blk=blk, emit_vn=not prompt, name=f"{tag}_gmlp")
        out_a = res[0]
        conv_args = (p["conv_w"], vec(p["conv_b"]), vec(p["conv_ln_g"]), vec(p["conv_ln_b"]), l)
        if prompt:
            out_b = _conv(glu, glu, *conv_args, rows=256, prev_is_glu=True, name=f"{tag}_conv")
        else:
            out_b = _conv(glu, hist[l], *conv_args, rows=t, prev_is_glu=False, name=f"{tag}_conv")
        new_conv = glu.reshape(bsz, t, br)[:, t - (CONV_WIDTH - 1):].astype(F32)
        qk = _rope(swa, br + kvw, tables, name=f"{tag}_rope")
        new = lambda a, heads: a.astype(F32).reshape(bsz, -1, heads, HEAD_DIM)
        if prompt:
            out_c = _attn_prompt(qk, 0, qk, br // kvw, swa, (br + kvw) // kvw, bias_c, sinks, l,
                                 tq=tq_c, nkb=nkb_c, n_heads=n_heads, group=n_heads // SWA_KV_HEADS,
                                 name=f"{tag}_swa")
            out_d = _attn_prompt(band, 0, band, 1, band, 2, bias_d, None, l,
                                 tq=tq_d, nkb=nkb_d, n_heads=n_heads, group=1, name=f"{tag}_band")
            swa_k = new(qk[m - SWA_PREV * CHUNK:, br:], SWA_KV_HEADS)
            swa_v = new(swa[m - SWA_PREV * CHUNK:, br + kvw:], SWA_KV_HEADS)
            band_k = new(band[m - BAND_PREV * CHUNK:, br:2 * br], n_heads)
            band_v = new(band[m - BAND_PREV * CHUNK:, 2 * br:], n_heads)
        else:
            out_c = _attn_sample(qk, 0, qk, br // kvw, swa, (br + kvw) // kvw, ck, cv, None, sinks, l,
                                 t=t, n_heads=n_heads, group=n_heads // SWA_KV_HEADS, name=f"{tag}_swa")
            out_d = _attn_sample(band, 0, band, 1, band, 2, bk, bv, bias_d, None, l,
                                 t=t, n_heads=n_heads, group=1, name=f"{tag}_band")
            keep = lambda c, n: jnp.concatenate([c[:, n.shape[1]:], n], axis=1)
            swa_k = keep(caches[1][l], new(qk[:, br:], SWA_KV_HEADS))
            swa_v = keep(caches[2][l], new(swa[:, br + kvw:], SWA_KV_HEADS))
            band_k = keep(caches[3][l], new(band[:, br:2 * br], n_heads))
            band_v = keep(caches[4][l], new(band[:, 2 * br:], n_heads))
        merged = _merge([out_a, out_b, out_c, out_d], gates, w["branch"], l, name=f"{tag}_merge")
        x, h2 = _mm_ln(merged, w["out"], l, x, g1, vec(p["ln1_g"]), vec(p["ln1_b"]), sc2, sh2,
                       alpha=alpha, name=f"{tag}_out_ln")
        hid = _mm(h2, w["ff1"], l, 0, w["ff1"].shape[2], act="relu2", name=f"{tag}_ff1")
        nxt = mods[min(l + 1, depth - 1)]
        x, h = _mm_ln(hid, w["ff2"], l, x, g2, vec(p["ln2_g"]), vec(p["ln2_b"]),
                      rowwise(nxt[1]), rowwise(nxt[0]), alpha=alpha, name=f"{tag}_ff2_ln")
        st = [new_conv, swa_k, swa_v, band_k, band_v]
        if not prompt:
            st.append(res[1].reshape(bsz, t, br))
        states.append(st)
    stacked = [jnp.stack(s, axis=0) for s in zip(*states)]
    return x.reshape(bsz, t, d), stacked


def _mxu_weights(w_ada, w_in, w_branch, w_out, w_ff1, w_ff2):
    br = w_branch.shape[2]
    kvw = SWA_KV_HEADS * HEAD_DIM
    sizes = dict(gmlp=2 * br, glu=2 * br, swa=br + 2 * kvw, band=3 * br)
    src, o = {}, 0
    for name in ("gmlp", "glu", "swa", "band"):
        src[name] = (o, o + sizes[name])
        o += sizes[name]
    src["gate"] = (o, w_in.shape[2])
    order = ("gmlp", "glu", "band", "gate", "swa")
    cols, o = {}, 0
    for name in order:
        cols[name] = o
        o += src[name][1] - src[name][0]
    w_in_b = jnp.concatenate([w_in[:, :, src[n][0]:src[n][1]] for n in order], axis=2).astype(BF)
    cast = lambda a: a.astype(BF)
    return {"in": w_in_b, "cols": cols, "ada": cast(w_ada), "branch": cast(w_branch),
            "out": cast(w_out), "ff1": cast(w_ff1), "ff2": cast(w_ff2)}


def kernel(x_prompt, x_sample, state_conv, cache_swa_k, cache_swa_v, cache_band_k, cache_band_v, c_prompt, c_sample, w_ada, b_ada, w_in, gmlp_ln_g, gmlp_ln_b, w_spatial, b_spatial, conv_w, conv_b, conv_ln_g, conv_ln_b, swa_sinks, band_rel_bias, w_branch, w_out, ln1_g, ln1_b, w_ff1, w_ff2, ln2_g, ln2_b):
    depth = w_in.shape[0]
    d = x_prompt.shape[2]
    p = dict(gmlp_ln_g=gmlp_ln_g, gmlp_ln_b=gmlp_ln_b, w_spatial=w_spatial, b_spatial=b_spatial,
             conv_w=conv_w, conv_b=conv_b, conv_ln_g=conv_ln_g, conv_ln_b=conv_ln_b,
             swa_sinks=swa_sinks, band_rel_bias=band_rel_bias, ln1_g=ln1_g, ln1_b=ln1_b,
             ln2_g=ln2_g, ln2_b=ln2_b)
    w = _mxu_weights(w_ada, w_in, w_branch, w_out, w_ff1, w_ff2)
    nb_p, nb_s = c_prompt.shape[0], c_sample.shape[0]
    pad = (-(nb_p + nb_s)) % SUBLANES
    c_all = jnp.concatenate([c_prompt, c_sample, jnp.zeros((pad, d), F32)], axis=0)
    mods_p, mods_s = [], []
    for l in range(depth):
        mod = _mm(c_all, w["ada"], l, 0, w_ada.shape[2], pre="silu", bias=b_ada[:, None, :],
                  out_dtype=F32, name="adaln")
        parts = jnp.split(mod, 6, axis=-1)
        mods_p.append([a[:nb_p] for a in parts])
        mods_s.append([a[nb_p:nb_p + nb_s] for a in parts])
    pos_p = jnp.arange(x_prompt.shape[1], dtype=jnp.int32)
    pos_s = PAST_LEN + jnp.arange(x_sample.shape[1], dtype=jnp.int32)
    y_p, st_p = _run_path("p", x_prompt, mods_p, None, pos_p, w, p)
    y_s, st_s = _run_path("s", x_sample, mods_s,
                          (state_conv, cache_swa_k, cache_swa_v, cache_band_k, cache_band_v),
                          pos_s, w, p)
    conv_p, swa_k_p, swa_v_p, band_k_p, band_v_p = st_p
    conv_s, swa_k_s, swa_v_s, band_k_s, band_v_s, gmlp_v_s = st_s
    return (y_p, y_s, conv_p, conv_s, swa_k_p, swa_v_p, swa_k_s, swa_v_s,
            band_k_p, band_v_p, band_k_s, band_v_s, gmlp_v_s)
```

```python
import functools

import jax
import jax.numpy as jnp
import numpy as np
from jax import lax
from jax.experimental import pallas as pl
from jax.experimental.pallas import tpu as pltpu

BF = jnp.bfloat16
F32 = jnp.float32

CHUNK = 64
HEAD_DIM = 64
GMLP_BLOCK = 128
GMLP_GROUPS = 8
CONV_WIDTH = 31
SWA_KV_HEADS = 4
SWA_PREV = 2
BAND_PREV = 8
MAX_REL = 256
PAST_LEN = 1024
ROPE_THETA = 500000.0
ROT_DIM = HEAD_DIM // 4
LN_EPS = 1e-5
NEG_INF = -1e30

LANES = 128
SUBLANES = 8
CONV_HALO = 32
VMEM_LIMIT = 56 * 1024 * 1024


def _params(*sem):
    return pltpu.CompilerParams(dimension_semantics=sem, vmem_limit_bytes=VMEM_LIMIT)


def _ln(x, g, b):
    mu = jnp.mean(x, axis=-1, keepdims=True)
    xc = x - mu
    var = jnp.mean(xc * xc, axis=-1, keepdims=True)
    return xc * lax.rsqrt(var + LN_EPS) * g + b


def _sigmoid(x):
    return jax.nn.sigmoid(x)


def _mm_kernel(*refs, act, pre, has_bias):
    a_ref, o_ref = refs[0], refs[-1]
    a = a_ref[...]
    if pre == "silu":
        a = a * _sigmoid(a)
    a = a.astype(BF)
    acc = jnp.dot(a, refs[1][...].astype(BF), preferred_element_type=F32)
    if has_bias:
        acc = acc + refs[2][...]
    if act == "gelu":
        acc = jax.nn.gelu(acc)
    elif act == "sigmoid":
        acc = _sigmoid(acc)
    elif act == "relu2":
        acc = jnp.square(jnp.maximum(acc, 0.0))
    elif act == "glu":
        acc = acc * _sigmoid(jnp.dot(a, refs[2][...].astype(BF), preferred_element_type=F32))
    o_ref[...] = acc.astype(o_ref.dtype)


def _mm(a, w, l, col0, n, *, name, act=None, pre=None, bias=None, out_dtype=BF, bm=1024, bn=1024):
    m, k = a.shape
    bm, bn = min(bm, m), min(bn, n)
    assert m % bm == 0 and n % bn == 0 and col0 % bn == 0, (m, n, bm, bn, col0)
    c0 = col0 // bn
    in_specs = [pl.BlockSpec((bm, k), lambda i, j: (i, 0)),
                pl.BlockSpec((None, k, bn), lambda i, j: (l, 0, c0 + j))]
    args = [a, w]
    if act == "glu":
        g0 = (col0 + n) // bn
        in_specs.append(pl.BlockSpec((None, k, bn), lambda i, j: (l, 0, g0 + j)))
        args.append(w)
    if bias is not None:
        in_specs.append(pl.BlockSpec((None, 1, bn), lambda i, j: (l, 0, j)))
        args.append(bias)
    return pl.pallas_call(
        functools.partial(_mm_kernel, act=act, pre=pre, has_bias=bias is not None),
        grid=(m // bm, n // bn),
        in_specs=in_specs,
        out_specs=pl.BlockSpec((bm, bn), lambda i, j: (i, j)),
        out_shape=jax.ShapeDtypeStruct((m, n), out_dtype),
        compiler_params=_params("parallel", "arbitrary"),
        name=name,
    )(*args)


LN_ROWS = 128


def _mm_ln_kernel(a_ref, w_ref, x_ref, g_ref, lng_ref, lnb_ref, sc_ref, sh_ref,
                  xo_ref, ho_ref, *, alpha, nk):
    k = pl.program_id(1)

    def rows_of(ref, rows):
        return ref[...] if ref.shape[0] == 1 else ref[rows, :]

    def finish(acc_rows):
        for r in range(xo_ref.shape[0] // LN_ROWS):
            rows = slice(r * LN_ROWS, (r + 1) * LN_ROWS)
            y = alpha * x_ref[rows, :] + (1.0 + rows_of(g_ref, rows)) * acc_rows(rows)
            xn = _ln(y, lng_ref[...], lnb_ref[...])
            xo_ref[rows, :] = xn
            ho_ref[rows, :] = (xn * (1.0 + rows_of(sc_ref, rows)) + rows_of(sh_ref, rows)).astype(ho_ref.dtype)

    part = jnp.dot(a_ref[...], w_ref[...], preferred_element_type=F32)
    if nk == 1:
        finish(lambda rows: part[rows, :])
        return

    @pl.when(k == 0)
    def _():
        xo_ref[...] = part

    @pl.when((k > 0) & (k < nk - 1))
    def _():
        xo_ref[...] += part

    @pl.when(k == nk - 1)
    def _():
        finish(lambda rows: xo_ref[rows, :] + part[rows, :])


def _row_spec(arr, bm, d):
    if arr.shape[0] == 1:
        return pl.BlockSpec((1, d), lambda i, k: (0, 0))
    return pl.BlockSpec((bm, d), lambda i, k: (i, 0))


def _mm_ln(a, w, l, x, gate, ln_g, ln_b, scale, shift, *, alpha, name, bm=512, bk=2048):
    m, kdim = a.shape
    d = w.shape[2]
    if gate.shape[0] != 1:
        bm //= 2
    bm, bk = min(bm, m), min(bk, kdim)
    assert m % bm == 0 and kdim % bk == 0 and bm % LN_ROWS == 0
    nk = kdim // bk
    const = pl.BlockSpec((None, 1, d), lambda i, k: (l, 0, 0))
    rows = pl.BlockSpec((bm, d), lambda i, k: (i, 0))
    return pl.pallas_call(
        functools.partial(_mm_ln_kernel, alpha=alpha, nk=nk),
        grid=(m // bm, nk),
        in_specs=[pl.BlockSpec((bm, bk), lambda i, k: (i, k)),
                  pl.BlockSpec((None, bk, d), lambda i, k: (l, k, 0)),
                  rows, _row_spec(gate, bm, d), const, const,
                  _row_spec(scale, bm, d), _row_spec(shift, bm, d)],
        out_specs=[rows, rows],
        out_shape=[jax.ShapeDtypeStruct((m, d), F32), jax.ShapeDtypeStruct((m, d), BF)],
        compiler_params=_params("parallel", "arbitrary"),
        name=name,
    )(a, w, x, gate, ln_g, ln_b, scale, shift)


def _mod_kernel(x_ref, sc_ref, sh_ref, o_ref):
    o_ref[...] = (x_ref[...] * (1.0 + sc_ref[...]) + sh_ref[...]).astype(o_ref.dtype)


def _modulate(x, scale, shift, *, name, bm=512):
    m, d = x.shape
    bm = min(bm, m)
    rows = pl.BlockSpec((bm, d), lambda i: (i, 0))

    def spec(arr):
        if arr.shape[0] == 1:
            return pl.BlockSpec((1, d), lambda i: (0, 0))
        return rows

    return pl.pallas_call(
        _mod_kernel, grid=(m // bm,),
        in_specs=[rows, spec(scale), spec(shift)],
        out_specs=rows,
        out_shape=jax.ShapeDtypeStruct((m, d), BF),
        compiler_params=_params("parallel"),
        name=name,
    )(x, scale, shift)


def _rope_kernel(x_ref, c_ref, sa_ref, sb_ref, o_ref):
    half = ROT_DIM // 2
    c, sa, sb = c_ref[...], sa_ref[...], sb_ref[...]
    for s in range(x_ref.shape[1] // LANES):
        x = x_ref[:, s * LANES:(s + 1) * LANES].astype(F32)
        up = pltpu.roll(x, LANES - half, axis=1)
        dn = pltpu.roll(x, half, axis=1)
        o_ref[:, s * LANES:(s + 1) * LANES] = (x * c + up * sa + dn * sb).astype(o_ref.dtype)


def _rope(slab, width, tables, *, name, bm=512):
    m = slab.shape[0]
    bm = min(bm, m)
    tab = pl.BlockSpec((bm, LANES), lambda i: (i, 0))
    return pl.pallas_call(
        _rope_kernel, grid=(m // bm,),
        in_specs=[pl.BlockSpec((bm, width), lambda i: (i, 0)), tab, tab, tab],
        out_specs=pl.BlockSpec((bm, width), lambda i: (i, 0)),
        out_shape=jax.ShapeDtypeStruct((m, width), BF),
        compiler_params=_params("parallel"),
        name=name,
    )(slab, *tables)


def _rope_tables(pos):
    half = ROT_DIM // 2
    inv_freq = jnp.power(jnp.float32(ROPE_THETA), -jnp.arange(half, dtype=F32) / half)
    ang = pos.astype(F32)[:, None] * inv_freq[None, :]
    cos, sin = jnp.cos(ang), jnp.sin(ang)
    t = pos.shape[0]
    ones = jnp.ones((t, HEAD_DIM - ROT_DIM), F32)
    zeros = jnp.zeros((t, HEAD_DIM - ROT_DIM), F32)
    zh = jnp.zeros((t, half), F32)
    c = jnp.concatenate([cos, cos, ones], axis=1)
    sa = jnp.concatenate([-sin, zh, zeros], axis=1)
    sb = jnp.concatenate([zh, sin, zeros], axis=1)
    rep = LANES // HEAD_DIM
    return tuple(jnp.tile(a, (1, rep)) for a in (c, sa, sb))


def _gmlp_kernel(ug_ref, ws_ref, bmap_ref, lng_ref, lnb_ref, o_ref, *vn_refs, blk):
    width = o_ref.shape[1]
    gc = width // GMLP_GROUPS
    t = lax.broadcasted_iota(jnp.int32, (GMLP_BLOCK, GMLP_BLOCK), 0)
    s = lax.broadcasted_iota(jnp.int32, (GMLP_BLOCK, GMLP_BLOCK), 1)
    mask = (s <= t) & (s >= (t // blk) * blk)
    ws = [jnp.where(mask, ws_ref[g], 0.0).astype(BF) for g in range(GMLP_GROUPS)]
    bmap = bmap_ref[...]
    for r in range(o_ref.shape[0] // GMLP_BLOCK):
        rows = slice(r * GMLP_BLOCK, (r + 1) * GMLP_BLOCK)
        vn = _ln(ug_ref[rows, width:].astype(F32), lng_ref[...], lnb_ref[...])
        if vn_refs:
            vn_refs[0][rows, :] = vn
        vnb = vn.astype(BF)
        for g in range(GMLP_GROUPS):
            cols = slice(g * gc, (g + 1) * gc)
            mixed = jnp.dot(ws[g], vnb[:, cols], preferred_element_type=F32)
            u = ug_ref[rows, cols].astype(F32)
            o_ref[rows, cols] = (u * (mixed + bmap[:, cols])).astype(o_ref.dtype)


def _gmlp(ug, ws, bmap, ln_g, ln_b, l, *, blk, emit_vn, name, bm=512):
    m, two_w = ug.shape
    width = two_w // 2
    bm = min(bm, m)
    vec = pl.BlockSpec((None, 1, width), lambda i: (l, 0, 0))
    out_specs = [pl.BlockSpec((bm, width), lambda i: (i, 0))]
    out_shape = [jax.ShapeDtypeStruct((m, width), BF)]
    if emit_vn:
        out_specs.append(pl.BlockSpec((bm, width), lambda i: (i, 0)))
        out_shape.append(jax.ShapeDtypeStruct((m, width), F32))
    return pl.pallas_call(
        functools.partial(_gmlp_kernel, blk=blk), grid=(m // bm,),
        in_specs=[pl.BlockSpec((bm, two_w), lambda i: (i, 0)),
                  pl.BlockSpec((None, GMLP_GROUPS, GMLP_BLOCK, GMLP_BLOCK), lambda i: (l, 0, 0, 0)),
                  pl.BlockSpec((None, GMLP_BLOCK, width), lambda i: (l, 0, 0)),
                  vec, vec],
        out_specs=out_specs, out_shape=out_shape,
        compiler_params=_params("parallel"),
        name=name,
    )(ug, ws, bmap, ln_g, ln_b)


CONV_ROWS = 32
CONV_COPY_ROWS = 56


def _conv_kernel(cur_ref, prev_ref, w_ref, cb_ref, lng_ref, lnb_ref, o_ref, xs_ref, *, zero_first):
    rows = cur_ref.shape[0]
    prev = prev_ref[...].astype(F32)
    if zero_first:
        prev = jnp.where(pl.program_id(0) == 0, 0.0, prev)
    xs_ref[0, 0:CONV_HALO, :] = prev
    xs_ref[0, CONV_HALO:CONV_HALO + rows, :] = cur_ref[...].astype(F32)
    n_shift = CONV_HALO + rows - SUBLANES
    for b in range(1, SUBLANES):
        for r0 in range(0, n_shift, CONV_COPY_ROWS):
            xs_ref[b, r0:r0 + CONV_COPY_ROWS, :] = xs_ref[0, r0 + b:r0 + b + CONV_COPY_ROWS, :]
    first = CONV_HALO - (CONV_WIDTH - 1)
    groups = CONV_ROWS // SUBLANES
    for c in range(rows // CONV_ROWS):
        accs = [jnp.zeros((SUBLANES, cur_ref.shape[1]), F32) for _ in range(groups)]
        for j in range(CONV_WIDTH):
            b = (first + j) % SUBLANES
            r0 = c * CONV_ROWS + first + j - b
            w = w_ref[j]
            for g in range(groups):
                rg = r0 + g * SUBLANES
                accs[g] = accs[g] + w * xs_ref[b, rg:rg + SUBLANES, :]
        acc = jnp.concatenate(accs, axis=0)
        y = _ln(acc + cb_ref[...], lng_ref[...], lnb_ref[...])
        o_ref[c * CONV_ROWS:(c + 1) * CONV_ROWS, :] = (y * _sigmoid(y)).astype(o_ref.dtype)


def _conv(glu, prev, conv_w, conv_b, ln_g, ln_b, l, *, rows, prev_is_glu, name):
    m, width = glu.shape
    rows = min(rows, m)
    step = rows // CONV_HALO
    if prev_is_glu:
        prev_spec = pl.BlockSpec((CONV_HALO, width), lambda i: (jnp.maximum(i * step - 1, 0), 0))
    else:
        prev_spec = pl.BlockSpec((CONV_HALO, width), lambda i: (i, 0))
    vec = pl.BlockSpec((None, 1, width), lambda i: (l, 0, 0))
    return pl.pallas_call(
        functools.partial(_conv_kernel, zero_first=prev_is_glu), grid=(m // rows,),
        in_specs=[pl.BlockSpec((rows, width), lambda i: (i, 0)), prev_spec,
                  pl.BlockSpec((None, CONV_WIDTH, SUBLANES, width), lambda i: (l, 0, 0, 0)),
                  vec, vec, vec],
        out_specs=pl.BlockSpec((rows, width), lambda i: (i, 0)),
        out_shape=jax.ShapeDtypeStruct((m, width), BF),
        scratch_shapes=[pltpu.VMEM((SUBLANES, CONV_HALO + rows, width), F32)],
        compiler_params=_params("parallel"),
        name=name,
    )(glu, prev, conv_w, conv_b, ln_g, ln_b)


def _half_mask(shape, half):
    lane = lax.broadcasted_iota(jnp.int32, shape, len(shape) - 1)
    return (lane >= HEAD_DIM) if half else (lane < HEAD_DIM)


def _stack_queries(q_ref, h0, per_slab, group):
    scale = HEAD_DIM ** -0.5
    blocks = []
    for qs in range(h0 // 2, (h0 + per_slab) // 2):
        q = (q_ref[:, qs * LANES:(qs + 1) * LANES].astype(F32) * scale).astype(BF)
        for q_half in range(2):
            k_half = ((2 * qs + q_half) // group) % 2
            qh = q if q_half == k_half else pltpu.roll(q, HEAD_DIM, axis=1)
            blocks.append(jnp.where(_half_mask(qh.shape, k_half), qh, jnp.zeros_like(qh)))
    return jnp.concatenate(blocks, axis=0)


def _attend_cols(q_ref, k_of, v_of, bias_of, sink_of, o_ref, *, n_heads, group):
    tq = q_ref.shape[0]
    per_slab = 2 * group
    n_slabs = n_heads // per_slab

    def scores(ks):
        qst = _stack_queries(q_ref, ks * per_slab, per_slab, group)
        s = lax.dot_general(k_of(ks), qst, (((1,), (1,)), ((), ())), preferred_element_type=F32)
        return bias_of(ks, s)

    s_next = scores(0)
    for ks in range(n_slabs):
        h0 = ks * per_slab
        s = s_next
        if ks + 1 < n_slabs:
            s_next = scores(ks + 1)
        m = jnp.max(s, axis=0, keepdims=True)
        if sink_of is not None:
            sink = jnp.concatenate([jnp.full((1, tq), sink_of(h), F32)
                                    for h in range(h0, h0 + per_slab)], axis=1)
            m = jnp.maximum(m, sink)
        p = jnp.exp(s - m)
        den = jnp.sum(p, axis=0, keepdims=True)
        if sink_of is not None:
            den = den + jnp.exp(sink - m)
        ot = lax.dot_general(v_of(ks), p.astype(BF), (((0,), (0,)), ((), ())),
                             preferred_element_type=F32)
        ot = ot / den
        for qs in range(h0 // 2, (h0 + per_slab) // 2):
            parts = []
            for q_half in range(2):
                h = 2 * qs + q_half
                k_half = (h // group) % 2
                parts.append(ot[k_half * HEAD_DIM:(k_half + 1) * HEAD_DIM, (h - h0) * tq:(h - h0 + 1) * tq])
            o_ref[:, qs * LANES:(qs + 1) * LANES] = jnp.concatenate(parts, axis=0).T.astype(o_ref.dtype)


def _attend(q_ref, k_of, v_of, bias_of, sink_of, o_ref, *, n_heads, group):
    tq = q_ref.shape[0]
    per_slab = 2 * group
    for ks in range(n_heads // per_slab):
        h0 = ks * per_slab
        qst = _stack_queries(q_ref, h0, per_slab, group)
        scores = [lax.dot_general(qst, kp, (((1,), (1,)), ((), ())), preferred_element_type=F32)
                  for kp in k_of(ks)]
        scores = [s.reshape(per_slab, tq, s.shape[1]) + b
                  for s, b in zip(scores, bias_of(h0, per_slab))]
        m = functools.reduce(jnp.maximum, [jnp.max(s, axis=-1, keepdims=True) for s in scores])
        if sink_of is not None:
            sink = jnp.concatenate([jnp.full((1, tq, 1), sink_of(h), F32)
                                    for h in range(h0, h0 + per_slab)], axis=0)
            m = jnp.maximum(m, sink)
        ps = [jnp.exp(s - m) for s in scores]
        den = functools.reduce(lambda a, b: a + b, [jnp.sum(p, axis=-1, keepdims=True) for p in ps])
        if sink_of is not None:
            den = den + jnp.exp(sink - m)
        o = functools.reduce(
            lambda a, b: a + b,
            [jnp.dot(p.reshape(per_slab * tq, p.shape[2]).astype(BF), vp, preferred_element_type=F32)
             for p, vp in zip(ps, v_of(ks))])
        o = o.reshape(per_slab, tq, LANES) / den
        for qs in range(h0 // 2, (h0 + per_slab) // 2):
            halves = []
            for q_half in range(2):
                h = 2 * qs + q_half
                oh = o[h - h0]
                if (h // group) % 2 != q_half:
                    oh = pltpu.roll(oh, HEAD_DIM, axis=1)
                halves.append(oh)
            out = jnp.where(_half_mask(halves[0].shape, 0), halves[0], halves[1])
            o_ref[:, qs * LANES:(qs + 1) * LANES] = out.astype(o_ref.dtype)


def _attn_prompt_kernel(*refs, nkb, n_heads, group, has_sink, layer):
    q_ref = refs[0]
    k_refs = refs[1:1 + nkb]
    v_refs = refs[1 + nkb:1 + 2 * nkb]
    bias_ref = refs[1 + 2 * nkb]
    sink_ref = refs[2 + 2 * nkb] if has_sink else None
    o_ref = refs[-1]
    tq = q_ref.shape[0]
    first_key = (nkb - 1 - pl.program_id(0)) * tq

    def k_of(s):
        return jnp.concatenate([r[:, s * LANES:(s + 1) * LANES] for r in k_refs], axis=0)

    def v_of(s):
        return jnp.concatenate([r[:, s * LANES:(s + 1) * LANES] for r in v_refs], axis=0)

    def bias_of(ks, s):
        b = bias_ref[ks] if bias_ref.shape[0] > 1 else bias_ref[0]
        key = lax.broadcasted_iota(jnp.int32, s.shape, 0)
        return jnp.where(key < first_key, NEG_INF, s + b)

    sink_of = (lambda h: sink_ref[layer, h]) if has_sink else None
    _attend_cols(q_ref, k_of, v_of, bias_of, sink_of, o_ref, n_heads=n_heads, group=group)


def _attn_prompt(q_arr, q_blk, k_arr, k_blk, v_arr, v_blk, bias, sinks, l, *, tq, nkb, n_heads, group, name):
    m = q_arr.shape[0]
    qw = n_heads * HEAD_DIM
    kw = (n_heads // group) * HEAD_DIM
    in_specs = [pl.BlockSpec((tq, qw), lambda i: (i, q_blk))]
    args = [q_arr]
    for arr, blk in ((k_arr, k_blk), (v_arr, v_blk)):
        for b in range(nkb):
            in_specs.append(pl.BlockSpec(
                (tq, kw), lambda i, b=b, blk=blk: (jnp.maximum(i - (nkb - 1) + b, 0), blk)))
            args.append(arr)
    in_specs.append(pl.BlockSpec((None,) + bias.shape[1:], lambda i: (l, 0, 0, 0)))
    args.append(bias)
    if sinks is not None:
        in_specs.append(pl.BlockSpec(memory_space=pltpu.SMEM))
        args.append(sinks)
    return pl.pallas_call(
        functools.partial(_attn_prompt_kernel, nkb=nkb, n_heads=n_heads, group=group,
                          has_sink=sinks is not None, layer=l),
        grid=(m // tq,), in_specs=in_specs,
        out_specs=pl.BlockSpec((tq, qw), lambda i: (i, 0)),
        out_shape=jax.ShapeDtypeStruct((m, qw), BF),
        compiler_params=_params("parallel"),
        name=name,
    )(*args)


def _rel_bias(table, n_q, n_k, offset):
    d_min, d_max = offset - (n_k - 1), offset + n_q - 1
    lo, hi = max(d_min, -MAX_REL), min(d_max, MAX_REL)
    lead = table.shape[:-1]
    ext = jnp.concatenate(
        [jnp.broadcast_to(table[..., :1], lead + (lo - d_min + 1,)),
         table[..., lo + MAX_REL:hi + MAX_REL + 1],
         jnp.broadcast_to(table[..., -1:], lead + (d_max - hi,))], axis=-1)
    length = n_q + n_k
    rev = ext[..., ::-1]
    flat = jnp.tile(rev, (1,) * len(lead) + (n_q,))[..., :n_q * (length - 1)]
    toep = flat.reshape(lead + (n_q, length - 1))
    return toep[..., n_q - 1:n_q - 1 + n_k]


def _band_mask(tq, nkb, n_prev):
    r = np.arange(tq)[:, None]
    s = np.arange(nkb * tq)[None, :]
    q_chunk = (nkb - 1) * (tq // CHUNK) + r // CHUNK
    k_chunk = s // CHUNK
    return (k_chunk <= q_chunk) & (k_chunk >= q_chunk - n_prev)


def _attn_sample_kernel(*refs, n_heads, group, has_bias, has_sink, layer):
    q_ref, kn_ref, vn_ref, kc_ref, vc_ref = refs[:5]
    rest = list(refs[5:-1])
    bias_ref = rest.pop(0) if has_bias else None
    sink_ref = rest.pop(0) if has_sink else None
    o_ref = refs[-1]
    rows = kc_ref.shape[0]

    def k_of(s):
        sl = slice(s * LANES, (s + 1) * LANES)
        return [kc_ref[:, sl].astype(BF), kn_ref[:, sl]]

    def v_of(s):
        sl = slice(s * LANES, (s + 1) * LANES)
        return [vc_ref[:, sl].astype(BF), vn_ref[:, sl]]

    def bias_of(h0, nh):
        if not has_bias:
            return [0.0, 0.0]
        return [bias_ref[h0:h0 + nh, :, :rows], bias_ref[h0:h0 + nh, :, rows:]]

    sink_of = (lambda h: sink_ref[layer, h]) if has_sink else None
    _attend(q_ref, k_of, v_of, bias_of, sink_of, o_ref, n_heads=n_heads, group=group)


def _attn_sample(q_arr, q_blk, k_arr, k_blk, v_arr, v_blk, k_cache, v_cache, bias, sinks, l, *,
                 t, n_heads, group, name):
    _, bsz, rows, kw = k_cache.shape
    qw = n_heads * HEAD_DIM
    in_specs = [pl.BlockSpec((t, qw), lambda i: (i, q_blk)),
                pl.BlockSpec((t, kw), lambda i: (i, k_blk)),
                pl.BlockSpec((t, kw), lambda i: (i, v_blk)),
                pl.BlockSpec((None, None, rows, kw), lambda i: (l, i, 0, 0)),
                pl.BlockSpec((None, None, rows, kw), lambda i: (l, i, 0, 0))]
    args = [q_arr, k_arr, v_arr, k_cache, v_cache]
    if bias is not None:
        in_specs.append(pl.BlockSpec((None,) + bias.shape[1:], lambda i: (l, 0, 0, 0)))
        args.append(bias)
    if sinks is not None:
        in_specs.append(pl.BlockSpec(memory_space=pltpu.SMEM))
        args.append(sinks)
    return pl.pallas_call(
        functools.partial(_attn_sample_kernel, n_heads=n_heads, group=group,
                          has_bias=bias is not None, has_sink=sinks is not None, layer=l),
        grid=(bsz,), in_specs=in_specs,
        out_specs=pl.BlockSpec((t, qw), lambda i: (i, 0)),
        out_shape=jax.ShapeDtypeStruct((bsz * t, qw), BF),
        compiler_params=_params("parallel"),
        name=name,
    )(*args)


def _merge_kernel(*refs, n_branch):
    br = refs[:n_branch]
    gt = refs[n_branch:2 * n_branch]
    wb = refs[2 * n_branch:3 * n_branch]
    o_ref = refs[-1]
    acc = None
    for b, g, w in zip(br, gt, wb):
        y = g[...].astype(F32) * jnp.dot(b[...], w[...], preferred_element_type=F32)
        acc = y if acc is None else acc + y
    o_ref[...] = acc.astype(o_ref.dtype)


def _merge(branches, gates, w_branch, l, *, name, bm=512, bn=1024):
    n_branch = len(branches)
    m, kdim = branches[0].shape
    d = w_branch.shape[3]
    bm, bn = min(bm, m), min(bn, d)
    nj = d // bn
    in_specs = [pl.BlockSpec((bm, kdim), lambda j, i: (i, 0)) for _ in range(n_branch)]
    in_specs += [pl.BlockSpec((bm, bn), lambda j, i, b=b: (i, b * nj + j)) for b in range(n_branch)]
    in_specs += [pl.BlockSpec((None, None, kdim, bn), lambda j, i, b=b: (l, b, 0, j)) for b in range(n_branch)]
    return pl.pallas_call(
        functools.partial(_merge_kernel, n_branch=n_branch),
        grid=(nj, m // bm), in_specs=in_specs,
        out_specs=pl.BlockSpec((bm, bn), lambda j, i: (i, j)),
        out_shape=jax.ShapeDtypeStruct((m, d), BF),
        compiler_params=_params("parallel", "arbitrary"),
        name=name,
    )(*branches, *([gates] * n_branch), *([w_branch] * n_branch))


def _run_path(tag, x3, mods, caches, pos, w, p):
    bsz, t, d = x3.shape
    m = bsz * t
    depth = w["in"].shape[0]
    br = d // 2
    n_heads = br // HEAD_DIM
    kvw = SWA_KV_HEADS * HEAD_DIM
    prompt = caches is None
    alpha = (2 * depth) ** 0.25
    col = w["cols"]

    def rowwise(a):
        return a if bsz == 1 else jnp.repeat(a, t, axis=0)

    def vec(a):
        return a[:, None, :]

    tables = _rope_tables(jnp.tile(pos, bsz))
    blk = min(t, GMLP_BLOCK)
    rep = GMLP_BLOCK // blk
    ws = jnp.tile(p["w_spatial"][:, :, :blk, :blk], (1, 1, rep, rep))
    bmap = jnp.repeat(jnp.swapaxes(jnp.tile(p["b_spatial"][:, :, :blk], (1, 1, rep)), 1, 2),
                      br // GMLP_GROUPS, axis=2)
    sinks = p["swa_sinks"]
    conv_w8 = jnp.broadcast_to(p["conv_w"][:, :, None, :], (depth, CONV_WIDTH, SUBLANES, br))
    if prompt:
        tq_c, nkb_c, tq_d, nkb_d = 128, 2, 128, 5
        per_c = 2 * n_heads // SWA_KV_HEADS
        bias_c = jnp.where(_band_mask(tq_c, nkb_c, SWA_PREV), 0.0, NEG_INF).astype(F32).T
        bias_c = jnp.broadcast_to(jnp.tile(bias_c, (1, per_c)), (depth, 1, nkb_c * tq_c, per_c * tq_c))
        bias_d = jnp.where(_band_mask(tq_d, nkb_d, BAND_PREV),
                           _rel_bias(p["band_rel_bias"], tq_d, nkb_d * tq_d, (nkb_d - 1) * tq_d).astype(F32),
                           NEG_INF)
        bias_d = bias_d.reshape(depth, n_heads // 2, 2 * tq_d, nkb_d * tq_d).swapaxes(2, 3)
    else:
        rows_d = caches[3].shape[2]
        bias_d = _rel_bias(p["band_rel_bias"], t, rows_d + t, rows_d).astype(F32)
        hist = jnp.pad(caches[0], ((0, 0), (0, 0), (CONV_HALO - (CONV_WIDTH - 1), 0), (0, 0)))
        hist = hist.reshape(depth, bsz * CONV_HALO, br)
        flat = lambda c: c.astype(BF).reshape(c.shape[:3] + (-1,))
        ck, cv, bk, bv = (flat(c) for c in caches[1:])

    x = x3.reshape(m, d)
    h = _modulate(x, rowwise(mods[0][1]), rowwise(mods[0][0]), name=f"{tag}_mod0")
    states = []
    for l in range(depth):
        _, _, g1, sh2, sc2, g2 = [rowwise(a) for a in mods[l]]
        wide = dict(bm=2048, bn=512)
        ug = _mm(h, w["in"], l, col["gmlp"], 2 * br, act="gelu", name=f"{tag}_proj_gmlp", **wide)
        glu = _mm(h, w["in"], l, col["glu"], br, act="glu", bn=512, name=f"{tag}_proj_glu")
        swa = _mm(h, w["in"], l, col["swa"], br + 2 * kvw, name=f"{tag}_proj_swa", **wide)
        band = _mm(h, w["in"], l, col["band"], 3 * br, name=f"{tag}_proj_band", **wide)
        gates = _mm(h, w["in"], l, col["gate"], 4 * d, act="sigmoid", name=f"{tag}_proj_gate", **wide)
        res = _gmlp(ug, ws, bmap, vec(p["gmlp_ln_g"]), vec(p["gmlp_ln_b"]), l,
                    blk=blk, emit_vn=not prompt, name=f"{tag}_gmlp")
        out_a = res[0]
        conv_args = (conv_w8, vec(p["conv_b"]), vec(p["conv_ln_g"]), vec(p["conv_ln_b"]), l)
        if prompt:
            out_b = _conv(glu, glu, *conv_args, rows=256, prev_is_glu=True, name=f"{tag}_conv")
        else:
            out_b = _conv(glu, hist[l], *conv_args, rows=t, prev_is_glu=False, name=f"{tag}_conv")
        new_conv = glu.reshape(bsz, t, br)[:, t - (CONV_WIDTH - 1):].astype(F32)
        qk = _rope(swa, br + kvw, tables, name=f"{tag}_rope")
        new = lambda a, heads: a.astype(F32).reshape(bsz, -1, heads, HEAD_DIM)
        if prompt:
            out_c = _attn_prompt(qk, 0, qk, br // kvw, swa, (br + kvw) // kvw, bias_c, sinks, l,
                                 tq=tq_c, nkb=nkb_c, n_heads=n_heads, group=n_heads // SWA_KV_HEADS,
                                 name=f"{tag}_swa")
            out_d = _attn_prompt(band, 0, band, 1, band, 2, bias_d, None, l,
                                 tq=tq_d, nkb=nkb_d, n_heads=n_heads, group=1, name=f"{tag}_band")
            swa_k = new(qk[m - SWA_PREV * CHUNK:, br:], SWA_KV_HEADS)
            swa_v = new(swa[m - SWA_PREV * CHUNK:, br + kvw:], SWA_KV_HEADS)
            band_k = new(band[m - BAND_PREV * CHUNK:, br:2 * br], n_heads)
            band_v = new(band[m - BAND_PREV * CHUNK:, 2 * br:], n_heads)
        else:
            out_c = _attn_sample(qk, 0, qk, br // kvw, swa, (br + kvw) // kvw, ck, cv, None, sinks, l,
                                 t=t, n_heads=n_heads, group=n_heads // SWA_KV_HEADS, name=f"{tag}_swa")
            out_d = _attn_sample(band, 0, band, 1, band, 2, bk, bv, bias_d, None, l,
                                 t=t, n_heads=n_heads, group=1, name=f"{tag}_band")
            swa_k = new(qk[:, br:], SWA_KV_HEADS)
            swa_v = new(swa[:, br + kvw:], SWA_KV_HEADS)
            band_k = new(band[:, br:2 * br], n_heads)
            band_v = new(band[:, 2 * br:], n_heads)
        merged = _merge([out_a, out_b, out_c, out_d], gates, w["branch"], l, name=f"{tag}_merge")
        x, h2 = _mm_ln(merged, w["out"], l, x, g1, vec(p["ln1_g"]), vec(p["ln1_b"]), sc2, sh2,
                       alpha=alpha, name=f"{tag}_out_ln")
        hid = _mm(h2, w["ff1"], l, 0, w["ff1"].shape[2], act="relu2", name=f"{tag}_ff1", **wide)
        nxt = mods[min(l + 1, depth - 1)]
        x, h = _mm_ln(hid, w["ff2"], l, x, g2, vec(p["ln2_g"]), vec(p["ln2_b"]),
                      rowwise(nxt[1]), rowwise(nxt[0]), alpha=alpha, name=f"{tag}_ff2_ln")
        st = [new_conv, swa_k, swa_v, band_k, band_v]
        if not prompt:
            st.append(res[1].reshape(bsz, t, br))
        states.append(st)
    stacked = [jnp.stack(s, axis=0) for s in zip(*states)]
    if not prompt:
        for i in range(1, 5):
            stacked[i] = jnp.concatenate([caches[i][:, :, t:], stacked[i]], axis=2)
    return x.reshape(bsz, t, d), stacked


def _mxu_weights(w_ada, w_in, w_branch, w_out, w_ff1, w_ff2):
    br = w_branch.shape[2]
    kvw = SWA_KV_HEADS * HEAD_DIM
    cols, o = {}, 0
    for name, size in (("gmlp", 2 * br), ("glu", 2 * br), ("swa", br + 2 * kvw), ("band", 3 * br)):
        cols[name] = o
        o += size
    cols["gate"] = o
    cast = lambda a: a.astype(BF)
    return {"in": w_in, "cols": cols, "ada": w_ada, "branch": cast(w_branch),
            "out": cast(w_out), "ff1": w_ff1, "ff2": cast(w_ff2)}


def kernel(x_prompt, x_sample, state_conv, cache_swa_k, cache_swa_v, cache_band_k, cache_band_v, c_prompt, c_sample, w_ada, b_ada, w_in, gmlp_ln_g, gmlp_ln_b, w_spatial, b_spatial, conv_w, conv_b, conv_ln_g, conv_ln_b, swa_sinks, band_rel_bias, w_branch, w_out, ln1_g, ln1_b, w_ff1, w_ff2, ln2_g, ln2_b):
    depth = w_in.shape[0]
    d = x_prompt.shape[2]
    p = dict(gmlp_ln_g=gmlp_ln_g, gmlp_ln_b=gmlp_ln_b, w_spatial=w_spatial, b_spatial=b_spatial,
             conv_w=conv_w, conv_b=conv_b, conv_ln_g=conv_ln_g, conv_ln_b=conv_ln_b,
             swa_sinks=swa_sinks, band_rel_bias=band_rel_bias, ln1_g=ln1_g, ln1_b=ln1_b,
             ln2_g=ln2_g, ln2_b=ln2_b)
    w = _mxu_weights(w_ada, w_in, w_branch, w_out, w_ff1, w_ff2)
    nb_p, nb_s = c_prompt.shape[0], c_sample.shape[0]
    pad = (-(nb_p + nb_s)) % SUBLANES
    c_all = jnp.concatenate([c_prompt, c_sample, jnp.zeros((pad, d), F32)], axis=0)
    mods_p, mods_s = [], []
    for l in range(depth):
        mod = _mm(c_all, w["ada"], l, 0, w_ada.shape[2], pre="silu", bias=b_ada[:, None, :],
                  out_dtype=F32, name="adaln")
        parts = jnp.split(mod, 6, axis=-1)
        mods_p.append([a[:nb_p] for a in parts])
        mods_s.append([a[nb_p:nb_p + nb_s] for a in parts])
    pos_p = jnp.arange(x_prompt.shape[1], dtype=jnp.int32)
    pos_s = PAST_LEN + jnp.arange(x_sample.shape[1], dtype=jnp.int32)
    y_p, st_p = _run_path("p", x_prompt, mods_p, None, pos_p, w, p)
    y_s, st_s = _run_path("s", x_sample, mods_s,
                          (state_conv, cache_swa_k, cache_swa_v, cache_band_k, cache_band_v),
                          pos_s, w, p)
    conv_p, swa_k_p, swa_v_p, band_k_p, band_v_p = st_p
    conv_s, swa_k_s, swa_v_s, band_k_s, band_v_s, gmlp_v_s = st_s
    return (y_p, y_s, conv_p, conv_s, swa_k_p, swa_v_p, swa_k_s, swa_v_s,
            band_k_p, band_v_p, band_k_s, band_v_s, gmlp_v_s)
```

```python
import functools

import jax
import jax.numpy as jnp
import numpy as np
from jax import lax
from jax.experimental import pallas as pl
from jax.experimental.pallas import tpu as pltpu

BF = jnp.bfloat16
F32 = jnp.float32

CHUNK = 64
HEAD_DIM = 64
GMLP_BLOCK = 128
GMLP_GROUPS = 8
CONV_WIDTH = 31
SWA_KV_HEADS = 4
SWA_PREV = 2
BAND_PREV = 8
MAX_REL = 256
PAST_LEN = 1024
ROPE_THETA = 500000.0
ROT_DIM = HEAD_DIM // 4
LN_EPS = 1e-5
NEG_INF = -1e30

LANES = 128
SUBLANES = 8
CONV_HALO = 32
VMEM_LIMIT = 56 * 1024 * 1024


def _params(*sem):
    return pltpu.CompilerParams(dimension_semantics=sem, vmem_limit_bytes=VMEM_LIMIT)


def _ln(x, g, b):
    mu = jnp.mean(x, axis=-1, keepdims=True)
    xc = x - mu
    var = jnp.mean(xc * xc, axis=-1, keepdims=True)
    return xc * lax.rsqrt(var + LN_EPS) * g + b


def _sigmoid(x):
    return 0.5 * jnp.tanh(0.5 * x) + 0.5


def _mm_kernel(*refs, act, pre, has_bias):
    a_ref, o_ref = refs[0], refs[-1]
    a = a_ref[...]
    if pre == "silu":
        a = a * _sigmoid(a)
    a = a.astype(BF)
    acc = jnp.dot(a, refs[1][...].astype(BF), preferred_element_type=F32)
    if has_bias:
        acc = acc + refs[2][...]
    if act == "gelu":
        acc = jax.nn.gelu(acc)
    elif act == "sigmoid":
        acc = _sigmoid(acc)
    elif act == "relu2":
        acc = jnp.square(jnp.maximum(acc, 0.0))
    elif act == "glu":
        acc = acc * _sigmoid(jnp.dot(a, refs[2][...].astype(BF), preferred_element_type=F32))
    o_ref[...] = acc.astype(o_ref.dtype)


def _mm(a, w, l, col0, n, *, name, act=None, pre=None, bias=None, out_dtype=BF, bm=1024, bn=1024):
    m, k = a.shape
    bm, bn = min(bm, m), min(bn, n)
    assert m % bm == 0 and n % bn == 0 and col0 % bn == 0, (m, n, bm, bn, col0)
    c0 = col0 // bn
    in_specs = [pl.BlockSpec((bm, k), lambda i, j: (i, 0)),
                pl.BlockSpec((None, k, bn), lambda i, j: (l, 0, c0 + j))]
    args = [a, w]
    if act == "glu":
        g0 = (col0 + n) // bn
        in_specs.append(pl.BlockSpec((None, k, bn), lambda i, j: (l, 0, g0 + j)))
        args.append(w)
    if bias is not None:
        in_specs.append(pl.BlockSpec((None, 1, bn), lambda i, j: (l, 0, j)))
        args.append(bias)
    return pl.pallas_call(
        functools.partial(_mm_kernel, act=act, pre=pre, has_bias=bias is not None),
        grid=(m // bm, n // bn),
        in_specs=in_specs,
        out_specs=pl.BlockSpec((bm, bn), lambda i, j: (i, j)),
        out_shape=jax.ShapeDtypeStruct((m, n), out_dtype),
        compiler_params=_params("parallel", "arbitrary"),
        name=name,
    )(*args)


LN_ROWS = 128


def _mm_ln_kernel(a_ref, w_ref, x_ref, g_ref, lng_ref, lnb_ref, sc_ref, sh_ref,
                  xo_ref, ho_ref, acc0_ref, acc1_ref, *, alpha, nk, n_tiles, seg):
    i, k = pl.program_id(0), pl.program_id(1)
    step = min(LN_ROWS, seg)

    def mod(ref, r):
        if ref.shape[0] == 1:
            return ref[...]
        b = (r * step) // seg
        return ref[b:b + 1, :]

    def epilogue(done_ref):
        for r in range(xo_ref.shape[0] // step):
            rows = slice(r * step, (r + 1) * step)
            y = alpha * x_ref[rows, :] + (1.0 + mod(g_ref, r)) * done_ref[rows, :]
            xn = _ln(y, lng_ref[...], lnb_ref[...])
            xo_ref[rows, :] = xn
            ho_ref[rows, :] = (xn * (1.0 + mod(sc_ref, r)) + mod(sh_ref, r)).astype(ho_ref.dtype)

    def matmul(acc_ref, first):
        part = jnp.dot(a_ref[...], w_ref[...], preferred_element_type=F32)
        if first:
            acc_ref[...] = part
        else:
            acc_ref[...] += part

    for parity, (cur_ref, done_ref) in enumerate(((acc0_ref, acc1_ref), (acc1_ref, acc0_ref))):
        mine = (i % 2) == parity

        @pl.when(mine & (i > 0) & (i < n_tiles) & (k == 0))
        def _():
            matmul(cur_ref, True)
            epilogue(done_ref)

        @pl.when(mine & (i < n_tiles) & (k > 0))
        def _():
            matmul(cur_ref, False)

        if parity == 0:
            @pl.when((i == 0) & (k == 0))
            def _():
                matmul(cur_ref, True)

        if n_tiles % 2 == parity:
            @pl.when((i == n_tiles) & (k == 0))
            def _():
                epilogue(done_ref)


def _mm_ln(a, w, l, x, gate, ln_g, ln_b, scale, shift, *, alpha, seg, name, bm=512, bk=2048):
    m, kdim = a.shape
    d = w.shape[2]
    bm, bk = min(bm, m), min(bk, kdim)
    assert m % bm == 0 and kdim % bk == 0 and bm % min(LN_ROWS, seg) == 0
    n_tiles, nk = m // bm, kdim // bk
    last = n_tiles - 1
    k_of = lambda i, k: jnp.where(i > last, nk - 1, k)
    done = lambda i, k: (jnp.maximum(i - 1, 0), 0)
    const = pl.BlockSpec((None, 1, d), lambda i, k: (l, 0, 0))
    rows = pl.BlockSpec((bm, d), done)

    def mod_spec(arr):
        if arr.shape[0] == 1:
            return pl.BlockSpec((1, d), lambda i, k: (0, 0))
        assert arr.shape[0] * seg == m and bm % seg == 0
        return pl.BlockSpec((bm // seg, d), done)

    return pl.pallas_call(
        functools.partial(_mm_ln_kernel, alpha=alpha, nk=nk, n_tiles=n_tiles, seg=seg),
        grid=(n_tiles + 1, nk),
        in_specs=[pl.BlockSpec((bm, bk), lambda i, k: (jnp.minimum(i, last), k_of(i, k))),
                  pl.BlockSpec((None, bk, d), lambda i, k: (l, k_of(i, k), 0)),
                  rows, mod_spec(gate), const, const, mod_spec(scale), mod_spec(shift)],
        out_specs=[rows, rows],
        out_shape=[jax.ShapeDtypeStruct((m, d), F32), jax.ShapeDtypeStruct((m, d), BF)],
        scratch_shapes=[pltpu.VMEM((bm, d), F32), pltpu.VMEM((bm, d), F32)],
        compiler_params=_params("arbitrary", "arbitrary"),
        name=name,
    )(a, w, x, gate, ln_g, ln_b, scale, shift)


def _mod_kernel(x_ref, sc_ref, sh_ref, o_ref):
    o_ref[...] = (x_ref[...] * (1.0 + sc_ref[...]) + sh_ref[...]).astype(o_ref.dtype)


def _modulate(x, scale, shift, *, name, bm=512):
    m, d = x.shape
    bm = min(bm, m)
    rows = pl.BlockSpec((bm, d), lambda i: (i, 0))

    def spec(arr):
        if arr.shape[0] == 1:
            return pl.BlockSpec((1, d), lambda i: (0, 0))
        return rows

    return pl.pallas_call(
        _mod_kernel, grid=(m // bm,),
        in_specs=[rows, spec(scale), spec(shift)],
        out_specs=rows,
        out_shape=jax.ShapeDtypeStruct((m, d), BF),
        compiler_params=_params("parallel"),
        name=name,
    )(x, scale, shift)


def _rope_kernel(x_ref, c_ref, sa_ref, sb_ref, o_ref):
    half = ROT_DIM // 2
    c, sa, sb = c_ref[...], sa_ref[...], sb_ref[...]
    for s in range(x_ref.shape[1] // LANES):
        x = x_ref[:, s * LANES:(s + 1) * LANES].astype(F32)
        up = pltpu.roll(x, LANES - half, axis=1)
        dn = pltpu.roll(x, half, axis=1)
        o_ref[:, s * LANES:(s + 1) * LANES] = (x * c + up * sa + dn * sb).astype(o_ref.dtype)


def _rope(slab, width, tables, *, name, bm=512):
    m = slab.shape[0]
    bm = min(bm, m)
    tab = pl.BlockSpec((bm, LANES), lambda i: (i, 0))
    return pl.pallas_call(
        _rope_kernel, grid=(m // bm,),
        in_specs=[pl.BlockSpec((bm, width), lambda i: (i, 0)), tab, tab, tab],
        out_specs=pl.BlockSpec((bm, width), lambda i: (i, 0)),
        out_shape=jax.ShapeDtypeStruct((m, width), BF),
        compiler_params=_params("parallel"),
        name=name,
    )(slab, *tables)


def _rope_tables(pos):
    half = ROT_DIM // 2
    inv_freq = jnp.power(jnp.float32(ROPE_THETA), -jnp.arange(half, dtype=F32) / half)
    ang = pos.astype(F32)[:, None] * inv_freq[None, :]
    cos, sin = jnp.cos(ang), jnp.sin(ang)
    t = pos.shape[0]
    ones = jnp.ones((t, HEAD_DIM - ROT_DIM), F32)
    zeros = jnp.zeros((t, HEAD_DIM - ROT_DIM), F32)
    zh = jnp.zeros((t, half), F32)
    c = jnp.concatenate([cos, cos, ones], axis=1)
    sa = jnp.concatenate([-sin, zh, zeros], axis=1)
    sb = jnp.concatenate([zh, sin, zeros], axis=1)
    rep = LANES // HEAD_DIM
    return tuple(jnp.tile(a, (1, rep)) for a in (c, sa, sb))


def _gmlp_kernel(ug_ref, ws_ref, bmap_ref, lng_ref, lnb_ref, o_ref, *vn_refs, blk):
    width = o_ref.shape[1]
    gc = width // GMLP_GROUPS
    t = lax.broadcasted_iota(jnp.int32, (GMLP_BLOCK, GMLP_BLOCK), 0)
    s = lax.broadcasted_iota(jnp.int32, (GMLP_BLOCK, GMLP_BLOCK), 1)
    mask = (s <= t) & (s >= (t // blk) * blk)
    ws = [jnp.where(mask, ws_ref[g], 0.0).astype(BF) for g in range(GMLP_GROUPS)]
    bmap = bmap_ref[...]
    for r in range(o_ref.shape[0] // GMLP_BLOCK):
        rows = slice(r * GMLP_BLOCK, (r + 1) * GMLP_BLOCK)
        vn = _ln(ug_ref[rows, width:].astype(F32), lng_ref[...], lnb_ref[...])
        if vn_refs:
            vn_refs[0][rows, :] = vn
        vnb = vn.astype(BF)
        for g in range(GMLP_GROUPS):
            cols = slice(g * gc, (g + 1) * gc)
            mixed = jnp.dot(ws[g], vnb[:, cols], preferred_element_type=F32)
            u = ug_ref[rows, cols].astype(F32)
            o_ref[rows, cols] = (u * (mixed + bmap[:, cols])).astype(o_ref.dtype)


def _gmlp(ug, ws, bmap, ln_g, ln_b, l, *, blk, emit_vn, name, bm=512):
    m, two_w = ug.shape
    width = two_w // 2
    bm = min(bm, m)
    vec = pl.BlockSpec((None, 1, width), lambda i: (l, 0, 0))
    out_specs = [pl.BlockSpec((bm, width), lambda i: (i, 0))]
    out_shape = [jax.ShapeDtypeStruct((m, width), BF)]
    if emit_vn:
        out_specs.append(pl.BlockSpec((bm, width), lambda i: (i, 0)))
        out_shape.append(jax.ShapeDtypeStruct((m, width), F32))
    return pl.pallas_call(
        functools.partial(_gmlp_kernel, blk=blk), grid=(m // bm,),
        in_specs=[pl.BlockSpec((bm, two_w), lambda i: (i, 0)),
                  pl.BlockSpec((None, GMLP_GROUPS, GMLP_BLOCK, GMLP_BLOCK), lambda i: (l, 0, 0, 0)),
                  pl.BlockSpec((None, GMLP_BLOCK, width), lambda i: (l, 0, 0)),
                  vec, vec],
        out_specs=out_specs, out_shape=out_shape,
        compiler_params=_params("parallel"),
        name=name,
    )(ug, ws, bmap, ln_g, ln_b)


CONV_ROWS = 32
CONV_COPY_ROWS = 56


def _conv_kernel(cur_ref, prev_ref, w_ref, cb_ref, lng_ref, lnb_ref, o_ref, xs_ref, *, zero_first):
    rows = cur_ref.shape[0]
    prev = prev_ref[...].astype(F32)
    if zero_first:
        prev = jnp.where(pl.program_id(0) == 0, 0.0, prev)
    xs_ref[0, 0:CONV_HALO, :] = prev
    xs_ref[0, CONV_HALO:CONV_HALO + rows, :] = cur_ref[...].astype(F32)
    n_shift = CONV_HALO + rows - SUBLANES
    for b in range(1, SUBLANES):
        for r0 in range(0, n_shift, CONV_COPY_ROWS):
            xs_ref[b, r0:r0 + CONV_COPY_ROWS, :] = xs_ref[0, r0 + b:r0 + b + CONV_COPY_ROWS, :]
    first = CONV_HALO - (CONV_WIDTH - 1)
    groups = CONV_ROWS // SUBLANES
    for c in range(rows // CONV_ROWS):
        accs = [jnp.zeros((SUBLANES, cur_ref.shape[1]), F32) for _ in range(groups)]
        for j in range(CONV_WIDTH):
            b = (first + j) % SUBLANES
            r0 = c * CONV_ROWS + first + j - b
            w = w_ref[j]
            for g in range(groups):
                rg = r0 + g * SUBLANES
                accs[g] = accs[g] + w * xs_ref[b, rg:rg + SUBLANES, :]
        acc = jnp.concatenate(accs, axis=0)
        y = _ln(acc + cb_ref[...], lng_ref[...], lnb_ref[...])
        o_ref[c * CONV_ROWS:(c + 1) * CONV_ROWS, :] = (y * _sigmoid(y)).astype(o_ref.dtype)


def _conv(glu, prev, conv_w, conv_b, ln_g, ln_b, l, *, rows, prev_is_glu, name):
    m, width = glu.shape
    rows = min(rows, m)
    step = rows // CONV_HALO
    if prev_is_glu:
        prev_spec = pl.BlockSpec((CONV_HALO, width), lambda i: (jnp.maximum(i * step - 1, 0), 0))
    else:
        prev_spec = pl.BlockSpec((CONV_HALO, width), lambda i: (i, 0))
    vec = pl.BlockSpec((None, 1, width), lambda i: (l, 0, 0))
    return pl.pallas_call(
        functools.partial(_conv_kernel, zero_first=prev_is_glu), grid=(m // rows,),
        in_specs=[pl.BlockSpec((rows, width), lambda i: (i, 0)), prev_spec,
                  pl.BlockSpec((None, CONV_WIDTH, SUBLANES, width), lambda i: (l, 0, 0, 0)),
                  vec, vec, vec],
        out_specs=pl.BlockSpec((rows, width), lambda i: (i, 0)),
        out_shape=jax.ShapeDtypeStruct((m, width), BF),
        scratch_shapes=[pltpu.VMEM((SUBLANES, CONV_HALO + rows, width), F32)],
        compiler_params=_params("parallel"),
        name=name,
    )(glu, prev, conv_w, conv_b, ln_g, ln_b)


def _half_mask(shape, half):
    lane = lax.broadcasted_iota(jnp.int32, shape, len(shape) - 1)
    return (lane >= HEAD_DIM) if half else (lane < HEAD_DIM)


def _stack_queries(q_ref, h0, per_slab, group):
    scale = HEAD_DIM ** -0.5
    blocks = []
    for qs in range(h0 // 2, (h0 + per_slab) // 2):
        q = (q_ref[:, qs * LANES:(qs + 1) * LANES].astype(F32) * scale).astype(BF)
        for q_half in range(2):
            k_half = ((2 * qs + q_half) // group) % 2
            qh = q if q_half == k_half else pltpu.roll(q, HEAD_DIM, axis=1)
            blocks.append(jnp.where(_half_mask(qh.shape, k_half), qh, jnp.zeros_like(qh)))
    return jnp.concatenate(blocks, axis=0)


def _attend_cols(q_ref, k_of, v_of, bias_of, sink_of, o_ref, *, n_heads, group):
    tq = q_ref.shape[0]
    per_slab = 2 * group
    n_slabs = n_heads // per_slab

    def scores(ks):
        qst = _stack_queries(q_ref, ks * per_slab, per_slab, group)
        s = lax.dot_general(k_of(ks), qst, (((1,), (1,)), ((), ())), preferred_element_type=F32)
        return bias_of(ks, s)

    s_next = scores(0)
    for ks in range(n_slabs):
        h0 = ks * per_slab
        s = s_next
        if ks + 1 < n_slabs:
            s_next = scores(ks + 1)
        m = jnp.max(s, axis=0, keepdims=True)
        if sink_of is not None:
            sink = jnp.concatenate([jnp.full((1, tq), sink_of(h), F32)
                                    for h in range(h0, h0 + per_slab)], axis=1)
            m = jnp.maximum(m, sink)
        p = jnp.exp(s - m)
        den = jnp.sum(p, axis=0, keepdims=True)
        if sink_of is not None:
            den = den + jnp.exp(sink - m)
        ot = lax.dot_general(v_of(ks), p.astype(BF), (((0,), (0,)), ((), ())),
                             preferred_element_type=F32)
        ot = ot / den
        for qs in range(h0 // 2, (h0 + per_slab) // 2):
            parts = []
            for q_half in range(2):
                h = 2 * qs + q_half
                k_half = (h // group) % 2
                parts.append(ot[k_half * HEAD_DIM:(k_half + 1) * HEAD_DIM, (h - h0) * tq:(h - h0 + 1) * tq])
            o_ref[:, qs * LANES:(qs + 1) * LANES] = jnp.concatenate(parts, axis=0).T.astype(o_ref.dtype)


def _attend(q_ref, k_of, v_of, bias_of, sink_of, o_ref, *, n_heads, group):
    tq = q_ref.shape[0]
    per_slab = 2 * group
    for ks in range(n_heads // per_slab):
        h0 = ks * per_slab
        qst = _stack_queries(q_ref, h0, per_slab, group)
        scores = [lax.dot_general(qst, kp, (((1,), (1,)), ((), ())), preferred_element_type=F32)
                  for kp in k_of(ks)]
        scores = [s.reshape(per_slab, tq, s.shape[1]) + b
                  for s, b in zip(scores, bias_of(h0, per_slab))]
        m = functools.reduce(jnp.maximum, [jnp.max(s, axis=-1, keepdims=True) for s in scores])
        if sink_of is not None:
            sink = jnp.concatenate([jnp.full((1, tq, 1), sink_of(h), F32)
                                    for h in range(h0, h0 + per_slab)], axis=0)
            m = jnp.maximum(m, sink)
        ps = [jnp.exp(s - m) for s in scores]
        den = functools.reduce(lambda a, b: a + b, [jnp.sum(p, axis=-1, keepdims=True) for p in ps])
        if sink_of is not None:
            den = den + jnp.exp(sink - m)
        o = functools.reduce(
            lambda a, b: a + b,
            [jnp.dot(p.reshape(per_slab * tq, p.shape[2]).astype(BF), vp, preferred_element_type=F32)
             for p, vp in zip(ps, v_of(ks))])
        o = o.reshape(per_slab, tq, LANES) / den
        for qs in range(h0 // 2, (h0 + per_slab) // 2):
            halves = []
            for q_half in range(2):
                h = 2 * qs + q_half
                oh = o[h - h0]
                if (h // group) % 2 != q_half:
                    oh = pltpu.roll(oh, HEAD_DIM, axis=1)
                halves.append(oh)
            out = jnp.where(_half_mask(halves[0].shape, 0), halves[0], halves[1])
            o_ref[:, qs * LANES:(qs + 1) * LANES] = out.astype(o_ref.dtype)


def _attn_prompt_kernel(*refs, nkb, n_heads, group, has_sink, layer):
    q_ref = refs[0]
    k_refs = refs[1:1 + nkb]
    v_refs = refs[1 + nkb:1 + 2 * nkb]
    bias_ref = refs[1 + 2 * nkb]
    sink_ref = refs[2 + 2 * nkb] if has_sink else None
    o_ref = refs[-1]
    tq = q_ref.shape[0]
    first_key = (nkb - 1 - pl.program_id(0)) * tq

    def k_of(s):
        return jnp.concatenate([r[:, s * LANES:(s + 1) * LANES] for r in k_refs], axis=0)

    def v_of(s):
        return jnp.concatenate([r[:, s * LANES:(s + 1) * LANES] for r in v_refs], axis=0)

    def bias_of(ks, s):
        b = bias_ref[ks] if bias_ref.shape[0] > 1 else bias_ref[0]
        key = lax.broadcasted_iota(jnp.int32, s.shape, 0)
        return jnp.where(key < first_key, NEG_INF, s + b)

    sink_of = (lambda h: sink_ref[layer, h]) if has_sink else None
    _attend_cols(q_ref, k_of, v_of, bias_of, sink_of, o_ref, n_heads=n_heads, group=group)


def _attn_prompt(q_arr, q_blk, k_arr, k_blk, v_arr, v_blk, bias, sinks, l, *, tq, nkb, n_heads, group, name):
    m = q_arr.shape[0]
    qw = n_heads * HEAD_DIM
    kw = (n_heads // group) * HEAD_DIM
    in_specs = [pl.BlockSpec((tq, qw), lambda i: (i, q_blk))]
    args = [q_arr]
    for arr, blk in ((k_arr, k_blk), (v_arr, v_blk)):
        for b in range(nkb):
            in_specs.append(pl.BlockSpec(
                (tq, kw), lambda i, b=b, blk=blk: (jnp.maximum(i - (nkb - 1) + b, 0), blk)))
            args.append(arr)
    in_specs.append(pl.BlockSpec((None,) + bias.shape[1:], lambda i: (l, 0, 0, 0)))
    args.append(bias)
    if sinks is not None:
        in_specs.append(pl.BlockSpec(memory_space=pltpu.SMEM))
        args.append(sinks)
    return pl.pallas_call(
        functools.partial(_attn_prompt_kernel, nkb=nkb, n_heads=n_heads, group=group,
                          has_sink=sinks is not None, layer=l),
        grid=(m // tq,), in_specs=in_specs,
        out_specs=pl.BlockSpec((tq, qw), lambda i: (i, 0)),
        out_shape=jax.ShapeDtypeStruct((m, qw), BF),
        compiler_params=_params("parallel"),
        name=name,
    )(*args)


def _rel_bias(table, n_q, n_k, offset):
    d_min, d_max = offset - (n_k - 1), offset + n_q - 1
    lo, hi = max(d_min, -MAX_REL), min(d_max, MAX_REL)
    lead = table.shape[:-1]
    ext = jnp.concatenate(
        [jnp.broadcast_to(table[..., :1], lead + (lo - d_min + 1,)),
         table[..., lo + MAX_REL:hi + MAX_REL + 1],
         jnp.broadcast_to(table[..., -1:], lead + (d_max - hi,))], axis=-1)
    length = n_q + n_k
    rev = ext[..., ::-1]
    flat = jnp.tile(rev, (1,) * len(lead) + (n_q,))[..., :n_q * (length - 1)]
    toep = flat.reshape(lead + (n_q, length - 1))
    return toep[..., n_q - 1:n_q - 1 + n_k]


def _band_mask(tq, nkb, n_prev):
    r = np.arange(tq)[:, None]
    s = np.arange(nkb * tq)[None, :]
    q_chunk = (nkb - 1) * (tq // CHUNK) + r // CHUNK
    k_chunk = s // CHUNK
    return (k_chunk <= q_chunk) & (k_chunk >= q_chunk - n_prev)


def _attn_sample_kernel(*refs, n_heads, group, has_bias, has_sink, layer):
    q_ref, kn_ref, vn_ref, kc_ref, vc_ref = refs[:5]
    rest = list(refs[5:-1])
    bias_ref = rest.pop(0) if has_bias else None
    sink_ref = rest.pop(0) if has_sink else None
    o_ref = refs[-1]
    rows = kc_ref.shape[0]

    def k_of(s):
        sl = slice(s * LANES, (s + 1) * LANES)
        return [kc_ref[:, sl].astype(BF), kn_ref[:, sl]]

    def v_of(s):
        sl = slice(s * LANES, (s + 1) * LANES)
        return [vc_ref[:, sl].astype(BF), vn_ref[:, sl]]

    def bias_of(h0, nh):
        if not has_bias:
            return [0.0, 0.0]
        return [bias_ref[h0:h0 + nh, :, :rows], bias_ref[h0:h0 + nh, :, rows:]]

    sink_of = (lambda h: sink_ref[layer, h]) if has_sink else None
    _attend(q_ref, k_of, v_of, bias_of, sink_of, o_ref, n_heads=n_heads, group=group)


def _attn_sample(q_arr, q_blk, k_arr, k_blk, v_arr, v_blk, k_cache, v_cache, bias, sinks, l, *,
                 t, n_heads, group, name):
    _, bsz, rows, kw = k_cache.shape
    qw = n_heads * HEAD_DIM
    in_specs = [pl.BlockSpec((t, qw), lambda i: (i, q_blk)),
                pl.BlockSpec((t, kw), lambda i: (i, k_blk)),
                pl.BlockSpec((t, kw), lambda i: (i, v_blk)),
                pl.BlockSpec((None, None, rows, kw), lambda i: (l, i, 0, 0)),
                pl.BlockSpec((None, None, rows, kw), lambda i: (l, i, 0, 0))]
    args = [q_arr, k_arr, v_arr, k_cache, v_cache]
    if bias is not None:
        in_specs.append(pl.BlockSpec((None,) + bias.shape[1:], lambda i: (l, 0, 0, 0)))
        args.append(bias)
    if sinks is not None:
        in_specs.append(pl.BlockSpec(memory_space=pltpu.SMEM))
        args.append(sinks)
    return pl.pallas_call(
        functools.partial(_attn_sample_kernel, n_heads=n_heads, group=group,
                          has_bias=bias is not None, has_sink=sinks is not None, layer=l),
        grid=(bsz,), in_specs=in_specs,
        out_specs=pl.BlockSpec((t, qw), lambda i: (i, 0)),
        out_shape=jax.ShapeDtypeStruct((bsz * t, qw), BF),
        compiler_params=_params("parallel"),
        name=name,
    )(*args)


def _merge_kernel(*refs, n_branch):
    br = refs[:n_branch]
    gt = refs[n_branch:2 * n_branch]
    wb = refs[2 * n_branch:3 * n_branch]
    o_ref = refs[-1]
    acc = None
    for b, g, w in zip(br, gt, wb):
        y = g[...].astype(F32) * jnp.dot(b[...], w[...], preferred_element_type=F32)
        acc = y if acc is None else acc + y
    o_ref[...] = acc.astype(o_ref.dtype)


def _merge(branches, gates, w_branch, l, *, name, bm=512, bn=1024):
    n_branch = len(branches)
    m, kdim = branches[0].shape
    d = w_branch.shape[3]
    bm, bn = min(bm, m), min(bn, d)
    nj = d // bn
    in_specs = [pl.BlockSpec((bm, kdim), lambda j, i: (i, 0)) for _ in range(n_branch)]
    in_specs += [pl.BlockSpec((bm, bn), lambda j, i, b=b: (i, b * nj + j)) for b in range(n_branch)]
    in_specs += [pl.BlockSpec((None, None, kdim, bn), lambda j, i, b=b: (l, b, 0, j)) for b in range(n_branch)]
    return pl.pallas_call(
        functools.partial(_merge_kernel, n_branch=n_branch),
        grid=(nj, m // bm), in_specs=in_specs,
        out_specs=pl.BlockSpec((bm, bn), lambda j, i: (i, j)),
        out_shape=jax.ShapeDtypeStruct((m, d), BF),
        compiler_params=_params("parallel", "arbitrary"),
        name=name,
    )(*branches, *([gates] * n_branch), *([w_branch] * n_branch))


def _roll_kernel(c_ref, n_ref, o_ref):
    rows, t = c_ref.shape[0], n_ref.shape[0]
    o_ref[0:rows - t] = c_ref[t:rows]
    o_ref[rows - t:rows] = n_ref[...]


def _roll_cache(cache, new, *, name):
    depth, bsz, rows, heads, hd = cache.shape
    blk = lambda r: pl.BlockSpec((None, None, r, heads, hd), lambda l, b: (l, b, 0, 0, 0))
    return pl.pallas_call(
        _roll_kernel, grid=(depth, bsz),
        in_specs=[blk(rows), blk(new.shape[2])], out_specs=blk(rows),
        out_shape=jax.ShapeDtypeStruct(cache.shape, cache.dtype),
        compiler_params=_params("parallel", "parallel"),
        name=name,
    )(cache, new)


def _run_path(tag, x3, mods, caches, pos, w, p):
    bsz, t, d = x3.shape
    m = bsz * t
    depth = w["in"].shape[0]
    br = d // 2
    n_heads = br // HEAD_DIM
    kvw = SWA_KV_HEADS * HEAD_DIM
    prompt = caches is None
    alpha = (2 * depth) ** 0.25
    col = w["cols"]

    def rowwise(a):
        return a if bsz == 1 else jnp.repeat(a, t, axis=0)

    def vec(a):
        return a[:, None, :]

    tables = _rope_tables(jnp.tile(pos, bsz))
    blk = min(t, GMLP_BLOCK)
    rep = GMLP_BLOCK // blk
    ws = jnp.tile(p["w_spatial"][:, :, :blk, :blk], (1, 1, rep, rep))
    bmap = jnp.repeat(jnp.swapaxes(jnp.tile(p["b_spatial"][:, :, :blk], (1, 1, rep)), 1, 2),
                      br // GMLP_GROUPS, axis=2)
    sinks = p["swa_sinks"]
    conv_w8 = jnp.broadcast_to(p["conv_w"][:, :, None, :], (depth, CONV_WIDTH, SUBLANES, br))
    if prompt:
        tq_c, nkb_c, tq_d, nkb_d = 128, 2, 128, 5
        per_c = 2 * n_heads // SWA_KV_HEADS
        bias_c = jnp.where(_band_mask(tq_c, nkb_c, SWA_PREV), 0.0, NEG_INF).astype(F32).T
        bias_c = jnp.broadcast_to(jnp.tile(bias_c, (1, per_c)), (depth, 1, nkb_c * tq_c, per_c * tq_c))
        bias_d = jnp.where(_band_mask(tq_d, nkb_d, BAND_PREV),
                           _rel_bias(p["band_rel_bias"], tq_d, nkb_d * tq_d, (nkb_d - 1) * tq_d).astype(F32),
                           NEG_INF)
        bias_d = bias_d.reshape(depth, n_heads // 2, 2 * tq_d, nkb_d * tq_d).swapaxes(2, 3)
    else:
        rows_d = caches[3].shape[2]
        bias_d = _rel_bias(p["band_rel_bias"], t, rows_d + t, rows_d).astype(F32)
        hist = jnp.pad(caches[0], ((0, 0), (0, 0), (CONV_HALO - (CONV_WIDTH - 1), 0), (0, 0)))
        hist = hist.reshape(depth, bsz * CONV_HALO, br)
        flat = lambda c: c.reshape(c.shape[:3] + (-1,))
        ck, cv, bk, bv = (flat(c) for c in caches[1:])

    x = x3.reshape(m, d)
    h = _modulate(x, rowwise(mods[0][1]), rowwise(mods[0][0]), name=f"{tag}_mod0")
    states = []
    for l in range(depth):
        _, _, g1, sh2, sc2, g2 = mods[l]
        wide = dict(bm=2048, bn=512)
        ug = _mm(h, w["in"], l, col["gmlp"], 2 * br, act="gelu", name=f"{tag}_proj_gmlp", **wide)
        glu = _mm(h, w["in"], l, col["glu"], br, act="glu", bn=512, name=f"{tag}_proj_glu")
        swa = _mm(h, w["in"], l, col["swa"], br + 2 * kvw, name=f"{tag}_proj_swa", **wide)
        band = _mm(h, w["in"], l, col["band"], 3 * br, name=f"{tag}_proj_band", **wide)
        gates = _mm(h, w["in"], l, col["gate"], 4 * d, act="sigmoid", name=f"{tag}_proj_gate", **wide)
        res = _gmlp(ug, ws, bmap, vec(p["gmlp_ln_g"]), vec(p["gmlp_ln_b"]), l,
                    blk=blk, emit_vn=not prompt, name=f"{tag}_gmlp")
        out_a = res[0]
        conv_args = (conv_w8, vec(p["conv_b"]), vec(p["conv_ln_g"]), vec(p["conv_ln_b"]), l)
        if prompt:
            out_b = _conv(glu, glu, *conv_args, rows=256, prev_is_glu=True, name=f"{tag}_conv")
        else:
            out_b = _conv(glu, hist[l], *conv_args, rows=t, prev_is_glu=False, name=f"{tag}_conv")
        new_conv = glu.reshape(bsz, t, br)[:, t - (CONV_WIDTH - 1):].astype(F32)
        qk = _rope(swa, br + kvw, tables, name=f"{tag}_rope")
        new = lambda a, heads: a.astype(F32).reshape(bsz, -1, heads, HEAD_DIM)
        if prompt:
            out_c = _attn_prompt(qk, 0, qk, br // kvw, swa, (br + kvw) // kvw, bias_c, sinks, l,
                                 tq=tq_c, nkb=nkb_c, n_heads=n_heads, group=n_heads // SWA_KV_HEADS,
                                 name=f"{tag}_swa")
            out_d = _attn_prompt(band, 0, band, 1, band, 2, bias_d, None, l,
                                 tq=tq_d, nkb=nkb_d, n_heads=n_heads, group=1, name=f"{tag}_band")
            swa_k = new(qk[m - SWA_PREV * CHUNK:, br:], SWA_KV_HEADS)
            swa_v = new(swa[m - SWA_PREV * CHUNK:, br + kvw:], SWA_KV_HEADS)
            band_k = new(band[m - BAND_PREV * CHUNK:, br:2 * br], n_heads)
            band_v = new(band[m - BAND_PREV * CHUNK:, 2 * br:], n_heads)
        else:
            out_c = _attn_sample(qk, 0, qk, br // kvw, swa, (br + kvw) // kvw, ck, cv, None, sinks, l,
                                 t=t, n_heads=n_heads, group=n_heads // SWA_KV_HEADS, name=f"{tag}_swa")
            out_d = _attn_sample(band, 0, band, 1, band, 2, bk, bv, bias_d, None, l,
                                 t=t, n_heads=n_heads, group=1, name=f"{tag}_band")
            swa_k = new(qk[:, br:], SWA_KV_HEADS)
            swa_v = new(swa[:, br + kvw:], SWA_KV_HEADS)
            band_k = new(band[:, br:2 * br], n_heads)
            band_v = new(band[:, 2 * br:], n_heads)
        merged = _merge([out_a, out_b, out_c, out_d], gates, w["branch"], l, name=f"{tag}_merge")
        x, h2 = _mm_ln(merged, w["out"], l, x, g1, vec(p["ln1_g"]), vec(p["ln1_b"]), sc2, sh2, seg=t,
                       alpha=alpha, name=f"{tag}_out_ln")
        hid = _mm(h2, w["ff1"], l, 0, w["ff1"].shape[2], act="relu2", name=f"{tag}_ff1", **wide)
        nxt = mods[min(l + 1, depth - 1)]
        x, h = _mm_ln(hid, w["ff2"], l, x, g2, vec(p["ln2_g"]), vec(p["ln2_b"]),
                      nxt[1], nxt[0], seg=t, alpha=alpha, name=f"{tag}_ff2_ln")
        st = [new_conv, swa_k, swa_v, band_k, band_v]
        if not prompt:
            st.append(res[1].reshape(bsz, t, br))
        states.append(st)
    stacked = [jnp.stack(s, axis=0) for s in zip(*states)]
    if not prompt:
        for i in range(1, 5):
            stacked[i] = _roll_cache(caches[i], stacked[i], name=f"{tag}_roll{i}")
    return x.reshape(bsz, t, d), stacked


def _mxu_weights(w_ada, w_in, w_branch, w_out, w_ff1, w_ff2):
    br = w_branch.shape[2]
    kvw = SWA_KV_HEADS * HEAD_DIM
    cols, o = {}, 0
    for name, size in (("gmlp", 2 * br), ("glu", 2 * br), ("swa", br + 2 * kvw), ("band", 3 * br)):
        cols[name] = o
        o += size
    cols["gate"] = o
    cast = lambda a: a.astype(BF)
    return {"in": w_in, "cols": cols, "ada": w_ada, "branch": cast(w_branch),
            "out": cast(w_out), "ff1": w_ff1, "ff2": cast(w_ff2)}


def kernel(x_prompt, x_sample, state_conv, cache_swa_k, cache_swa_v, cache_band_k, cache_band_v, c_prompt, c_sample, w_ada, b_ada, w_in, gmlp_ln_g, gmlp_ln_b, w_spatial, b_spatial, conv_w, conv_b, conv_ln_g, conv_ln_b, swa_sinks, band_rel_bias, w_branch, w_out, ln1_g, ln1_b, w_ff1, w_ff2, ln2_g, ln2_b):
    depth = w_in.shape[0]
    d = x_prompt.shape[2]
    p = dict(gmlp_ln_g=gmlp_ln_g, gmlp_ln_b=gmlp_ln_b, w_spatial=w_spatial, b_spatial=b_spatial,
             conv_w=conv_w, conv_b=conv_b, conv_ln_g=conv_ln_g, conv_ln_b=conv_ln_b,
             swa_sinks=swa_sinks, band_rel_bias=band_rel_bias, ln1_g=ln1_g, ln1_b=ln1_b,
             ln2_g=ln2_g, ln2_b=ln2_b)
    w = _mxu_weights(w_ada, w_in, w_branch, w_out, w_ff1, w_ff2)
    nb_p, nb_s = c_prompt.shape[0], c_sample.shape[0]
    pad = (-(nb_p + nb_s)) % SUBLANES
    c_all = jnp.concatenate([c_prompt, c_sample, jnp.zeros((pad, d), F32)], axis=0)
    mods_p, mods_s = [], []
    for l in range(depth):
        mod = _mm(c_all, w["ada"], l, 0, w_ada.shape[2], pre="silu", bias=b_ada[:, None, :],
                  out_dtype=F32, name="adaln")
        parts = jnp.split(mod, 6, axis=-1)
        mods_p.append([a[:nb_p] for a in parts])
        mods_s.append([a[nb_p:nb_p + nb_s] for a in parts])
    pos_p = jnp.arange(x_prompt.shape[1], dtype=jnp.int32)
    pos_s = PAST_LEN + jnp.arange(x_sample.shape[1], dtype=jnp.int32)
    y_p, st_p = _run_path("p", x_prompt, mods_p, None, pos_p, w, p)
    y_s, st_s = _run_path("s", x_sample, mods_s,
                          (state_conv, cache_swa_k, cache_swa_v, cache_band_k, cache_band_v),
                          pos_s, w, p)
    conv_p, swa_k_p, swa_v_p, band_k_p, band_v_p = st_p
    conv_s, swa_k_s, swa_v_s, band_k_s, band_v_s, gmlp_v_s = st_s
    return (y_p, y_s, conv_p, conv_s, swa_k_p, swa_v_p, swa_k_s, swa_v_s,
            band_k_p, band_v_p, band_k_s, band_v_s, gmlp_v_s)
```

```python
import functools

import jax
import jax.numpy as jnp
import numpy as np
from jax import lax
from jax.experimental import pallas as pl
from jax.experimental.pallas import tpu as pltpu

BF = jnp.bfloat16
F32 = jnp.float32

CHUNK = 64
HEAD_DIM = 64
GMLP_BLOCK = 128
GMLP_GROUPS = 8
CONV_WIDTH = 31
SWA_KV_HEADS = 4
SWA_PREV = 2
BAND_PREV = 8
MAX_REL = 256
PAST_LEN = 1024
ROPE_THETA = 500000.0
ROT_DIM = HEAD_DIM // 4
LN_EPS = 1e-5
NEG_INF = -1e30

LANES = 128
SUBLANES = 8
CONV_HALO = 32
VMEM_LIMIT = 56 * 1024 * 1024


def _params(*sem):
    return pltpu.CompilerParams(dimension_semantics=sem, vmem_limit_bytes=VMEM_LIMIT)


def _ln(x, g, b):
    mu = jnp.mean(x, axis=-1, keepdims=True)
    xc = x - mu
    var = jnp.mean(xc * xc, axis=-1, keepdims=True)
    return xc * lax.rsqrt(var + LN_EPS) * g + b


def _sigmoid(x):
    return 0.5 * jnp.tanh(0.5 * x) + 0.5


def _mm_kernel(*refs, act, pre, has_bias):
    a_ref, o_ref = refs[0], refs[-1]
    a = a_ref[...]
    if pre == "silu":
        a = a * _sigmoid(a)
    a = a.astype(BF)
    acc = jnp.dot(a, refs[1][...].astype(BF), preferred_element_type=F32)
    if has_bias:
        acc = acc + refs[2][...]
    if act == "gelu":
        acc = jax.nn.gelu(acc)
    elif act == "sigmoid":
        acc = _sigmoid(acc)
    elif act == "relu2":
        acc = jnp.square(jnp.maximum(acc, 0.0))
    elif act == "glu":
        acc = acc * _sigmoid(jnp.dot(a, refs[2][...].astype(BF), preferred_element_type=F32))
    o_ref[...] = acc.astype(o_ref.dtype)


def _mm(a, w, l, col0, n, *, name, act=None, pre=None, bias=None, out_dtype=BF, bm=1024, bn=1024):
    m, k = a.shape
    bm, bn = min(bm, m), min(bn, n)
    assert m % bm == 0 and n % bn == 0 and col0 % bn == 0, (m, n, bm, bn, col0)
    c0 = col0 // bn
    in_specs = [pl.BlockSpec((bm, k), lambda i, j: (i, 0)),
                pl.BlockSpec((None, k, bn), lambda i, j: (l, 0, c0 + j))]
    args = [a, w]
    if act == "glu":
        g0 = (col0 + n) // bn
        in_specs.append(pl.BlockSpec((None, k, bn), lambda i, j: (l, 0, g0 + j)))
        args.append(w)
    if bias is not None:
        in_specs.append(pl.BlockSpec((None, 1, bn), lambda i, j: (l, 0, j)))
        args.append(bias)
    return pl.pallas_call(
        functools.partial(_mm_kernel, act=act, pre=pre, has_bias=bias is not None),
        grid=(m // bm, n // bn),
        in_specs=in_specs,
        out_specs=pl.BlockSpec((bm, bn), lambda i, j: (i, j)),
        out_shape=jax.ShapeDtypeStruct((m, n), out_dtype),
        compiler_params=_params("parallel", "arbitrary"),
        name=name,
    )(*args)


LN_ROWS = 128


def _mm_ln_kernel(a_ref, w_ref, x_ref, g_ref, lng_ref, lnb_ref, sc_ref, sh_ref,
                  xo_ref, ho_ref, acc0_ref, acc1_ref, *, alpha, nk, n_tiles, seg):
    i, k = pl.program_id(0), pl.program_id(1)
    step = min(LN_ROWS, seg)

    def mod(ref, r):
        if ref.shape[0] == 1:
            return ref[...]
        b = (r * step) // seg
        return ref[b:b + 1, :]

    def epilogue(done_ref):
        for r in range(xo_ref.shape[0] // step):
            rows = slice(r * step, (r + 1) * step)
            y = alpha * x_ref[rows, :] + (1.0 + mod(g_ref, r)) * done_ref[rows, :]
            xn = _ln(y, lng_ref[...], lnb_ref[...])
            xo_ref[rows, :] = xn
            ho_ref[rows, :] = (xn * (1.0 + mod(sc_ref, r)) + mod(sh_ref, r)).astype(ho_ref.dtype)

    def matmul(acc_ref, first):
        part = jnp.dot(a_ref[...], w_ref[...], preferred_element_type=F32)
        if first:
            acc_ref[...] = part
        else:
            acc_ref[...] += part

    for parity, (cur_ref, done_ref) in enumerate(((acc0_ref, acc1_ref), (acc1_ref, acc0_ref))):
        mine = (i % 2) == parity

        @pl.when(mine & (i > 0) & (i < n_tiles) & (k == 0))
        def _():
            matmul(cur_ref, True)
            epilogue(done_ref)

        @pl.when(mine & (i < n_tiles) & (k > 0))
        def _():
            matmul(cur_ref, False)

        if parity == 0:
            @pl.when((i == 0) & (k == 0))
            def _():
                matmul(cur_ref, True)

        if n_tiles % 2 == parity:
            @pl.when((i == n_tiles) & (k == 0))
            def _():
                epilogue(done_ref)


def _mm_ln(a, w, l, x, gate, ln_g, ln_b, scale, shift, *, alpha, seg, name, bm=512, bk=2048):
    m, kdim = a.shape
    d = w.shape[2]
    bm, bk = min(bm, m), min(bk, kdim)
    assert m % bm == 0 and kdim % bk == 0 and bm % min(LN_ROWS, seg) == 0
    n_tiles, nk = m // bm, kdim // bk
    last = n_tiles - 1
    k_of = lambda i, k: jnp.where(i > last, nk - 1, k)
    done = lambda i, k: (jnp.maximum(i - 1, 0), 0)
    const = pl.BlockSpec((None, 1, d), lambda i, k: (l, 0, 0))
    rows = pl.BlockSpec((bm, d), done)

    def mod_spec(arr):
        if arr.shape[0] == 1:
            return pl.BlockSpec((1, d), lambda i, k: (0, 0))
        assert arr.shape[0] * seg == m and bm % seg == 0
        return pl.BlockSpec((bm // seg, d), done)

    return pl.pallas_call(
        functools.partial(_mm_ln_kernel, alpha=alpha, nk=nk, n_tiles=n_tiles, seg=seg),
        grid=(n_tiles + 1, nk),
        in_specs=[pl.BlockSpec((bm, bk), lambda i, k: (jnp.minimum(i, last), k_of(i, k))),
                  pl.BlockSpec((None, bk, d), lambda i, k: (l, k_of(i, k), 0)),
                  rows, mod_spec(gate), const, const, mod_spec(scale), mod_spec(shift)],
        out_specs=[rows, rows],
        out_shape=[jax.ShapeDtypeStruct((m, d), F32), jax.ShapeDtypeStruct((m, d), BF)],
        scratch_shapes=[pltpu.VMEM((bm, d), F32), pltpu.VMEM((bm, d), F32)],
        compiler_params=_params("arbitrary", "arbitrary"),
        name=name,
    )(a, w, x, gate, ln_g, ln_b, scale, shift)


def _mod_kernel(x_ref, sc_ref, sh_ref, o_ref):
    o_ref[...] = (x_ref[...] * (1.0 + sc_ref[...]) + sh_ref[...]).astype(o_ref.dtype)


def _modulate(x, scale, shift, *, name, bm=512):
    m, d = x.shape
    bm = min(bm, m)
    rows = pl.BlockSpec((bm, d), lambda i: (i, 0))

    def spec(arr):
        if arr.shape[0] == 1:
            return pl.BlockSpec((1, d), lambda i: (0, 0))
        return rows

    return pl.pallas_call(
        _mod_kernel, grid=(m // bm,),
        in_specs=[rows, spec(scale), spec(shift)],
        out_specs=rows,
        out_shape=jax.ShapeDtypeStruct((m, d), BF),
        compiler_params=_params("parallel"),
        name=name,
    )(x, scale, shift)


def _rope_kernel(x_ref, c_ref, sa_ref, sb_ref, o_ref):
    half = ROT_DIM // 2
    c, sa, sb = c_ref[...], sa_ref[...], sb_ref[...]
    for s in range(x_ref.shape[1] // LANES):
        x = x_ref[:, s * LANES:(s + 1) * LANES].astype(F32)
        up = pltpu.roll(x, LANES - half, axis=1)
        dn = pltpu.roll(x, half, axis=1)
        o_ref[:, s * LANES:(s + 1) * LANES] = (x * c + up * sa + dn * sb).astype(o_ref.dtype)


def _rope(slab, width, tables, *, name, bm=512):
    m = slab.shape[0]
    bm = min(bm, m)
    tab = pl.BlockSpec((bm, LANES), lambda i: (i, 0))
    return pl.pallas_call(
        _rope_kernel, grid=(m // bm,),
        in_specs=[pl.BlockSpec((bm, width), lambda i: (i, 0)), tab, tab, tab],
        out_specs=pl.BlockSpec((bm, width), lambda i: (i, 0)),
        out_shape=jax.ShapeDtypeStruct((m, width), BF),
        compiler_params=_params("parallel"),
        name=name,
    )(slab, *tables)


def _rope_tables(pos):
    half = ROT_DIM // 2
    inv_freq = jnp.power(jnp.float32(ROPE_THETA), -jnp.arange(half, dtype=F32) / half)
    ang = pos.astype(F32)[:, None] * inv_freq[None, :]
    cos, sin = jnp.cos(ang), jnp.sin(ang)
    t = pos.shape[0]
    ones = jnp.ones((t, HEAD_DIM - ROT_DIM), F32)
    zeros = jnp.zeros((t, HEAD_DIM - ROT_DIM), F32)
    zh = jnp.zeros((t, half), F32)
    c = jnp.concatenate([cos, cos, ones], axis=1)
    sa = jnp.concatenate([-sin, zh, zeros], axis=1)
    sb = jnp.concatenate([zh, sin, zeros], axis=1)
    rep = LANES // HEAD_DIM
    return tuple(jnp.tile(a, (1, rep)) for a in (c, sa, sb))


def _gmlp_kernel(ug_ref, ws_ref, bmap_ref, lng_ref, lnb_ref, o_ref, *vn_refs, blk):
    width = o_ref.shape[1]
    gc = width // GMLP_GROUPS
    t = lax.broadcasted_iota(jnp.int32, (GMLP_BLOCK, GMLP_BLOCK), 0)
    s = lax.broadcasted_iota(jnp.int32, (GMLP_BLOCK, GMLP_BLOCK), 1)
    mask = (s <= t) & (s >= (t // blk) * blk)
    ws = [jnp.where(mask, ws_ref[g], 0.0).astype(BF) for g in range(GMLP_GROUPS)]
    bmap = bmap_ref[...]
    for r in range(o_ref.shape[0] // GMLP_BLOCK):
        rows = slice(r * GMLP_BLOCK, (r + 1) * GMLP_BLOCK)
        vn = _ln(ug_ref[rows, width:].astype(F32), lng_ref[...], lnb_ref[...])
        if vn_refs:
            vn_refs[0][rows, :] = vn
        vnb = vn.astype(BF)
        for g in range(GMLP_GROUPS):
            cols = slice(g * gc, (g + 1) * gc)
            mixed = jnp.dot(ws[g], vnb[:, cols], preferred_element_type=F32)
            u = ug_ref[rows, cols].astype(F32)
            o_ref[rows, cols] = (u * (mixed + bmap[:, cols])).astype(o_ref.dtype)


def _gmlp(ug, ws, bmap, ln_g, ln_b, l, *, blk, emit_vn, name, bm=512):
    m, two_w = ug.shape
    width = two_w // 2
    bm = min(bm, m)
    vec = pl.BlockSpec((None, 1, width), lambda i: (l, 0, 0))
    out_specs = [pl.BlockSpec((bm, width), lambda i: (i, 0))]
    out_shape = [jax.ShapeDtypeStruct((m, width), BF)]
    if emit_vn:
        out_specs.append(pl.BlockSpec((bm, width), lambda i: (i, 0)))
        out_shape.append(jax.ShapeDtypeStruct((m, width), F32))
    return pl.pallas_call(
        functools.partial(_gmlp_kernel, blk=blk), grid=(m // bm,),
        in_specs=[pl.BlockSpec((bm, two_w), lambda i: (i, 0)),
                  pl.BlockSpec((None, GMLP_GROUPS, GMLP_BLOCK, GMLP_BLOCK), lambda i: (l, 0, 0, 0)),
                  pl.BlockSpec((None, GMLP_BLOCK, width), lambda i: (l, 0, 0)),
                  vec, vec],
        out_specs=out_specs, out_shape=out_shape,
        compiler_params=_params("parallel"),
        name=name,
    )(ug, ws, bmap, ln_g, ln_b)


CONV_ROWS = 32
CONV_COPY_ROWS = 56


def _conv_kernel(cur_ref, prev_ref, w_ref, cb_ref, lng_ref, lnb_ref, o_ref, xs_ref, *, zero_first):
    rows = cur_ref.shape[0]
    prev = prev_ref[...].astype(F32)
    if zero_first:
        prev = jnp.where(pl.program_id(0) == 0, 0.0, prev)
    xs_ref[0, 0:CONV_HALO, :] = prev
    xs_ref[0, CONV_HALO:CONV_HALO + rows, :] = cur_ref[...].astype(F32)
    n_shift = CONV_HALO + rows - SUBLANES
    for b in range(1, SUBLANES):
        for r0 in range(0, n_shift, CONV_COPY_ROWS):
            xs_ref[b, r0:r0 + CONV_COPY_ROWS, :] = xs_ref[0, r0 + b:r0 + b + CONV_COPY_ROWS, :]
    first = CONV_HALO - (CONV_WIDTH - 1)
    groups = CONV_ROWS // SUBLANES
    for c in range(rows // CONV_ROWS):
        accs = [jnp.zeros((SUBLANES, cur_ref.shape[1]), F32) for _ in range(groups)]
        for j in range(CONV_WIDTH):
            b = (first + j) % SUBLANES
            r0 = c * CONV_ROWS + first + j - b
            w = w_ref[j]
            for g in range(groups):
                rg = r0 + g * SUBLANES
                accs[g] = accs[g] + w * xs_ref[b, rg:rg + SUBLANES, :]
        acc = jnp.concatenate(accs, axis=0)
        y = _ln(acc + cb_ref[...], lng_ref[...], lnb_ref[...])
        o_ref[c * CONV_ROWS:(c + 1) * CONV_ROWS, :] = (y * _sigmoid(y)).astype(o_ref.dtype)


def _conv(glu, prev, conv_w, conv_b, ln_g, ln_b, l, *, rows, prev_is_glu, name):
    m, width = glu.shape
    rows = min(rows, m)
    step = rows // CONV_HALO
    if prev_is_glu:
        prev_spec = pl.BlockSpec((CONV_HALO, width), lambda i: (jnp.maximum(i * step - 1, 0), 0))
    else:
        prev_spec = pl.BlockSpec((CONV_HALO, width), lambda i: (i, 0))
    vec = pl.BlockSpec((None, 1, width), lambda i: (l, 0, 0))
    return pl.pallas_call(
        functools.partial(_conv_kernel, zero_first=prev_is_glu), grid=(m // rows,),
        in_specs=[pl.BlockSpec((rows, width), lambda i: (i, 0)), prev_spec,
                  pl.BlockSpec((None, CONV_WIDTH, SUBLANES, width), lambda i: (l, 0, 0, 0)),
                  vec, vec, vec],
        out_specs=pl.BlockSpec((rows, width), lambda i: (i, 0)),
        out_shape=jax.ShapeDtypeStruct((m, width), BF),
        scratch_shapes=[pltpu.VMEM((SUBLANES, CONV_HALO + rows, width), F32)],
        compiler_params=_params("parallel"),
        name=name,
    )(glu, prev, conv_w, conv_b, ln_g, ln_b)


def _half_mask(shape, half):
    lane = lax.broadcasted_iota(jnp.int32, shape, len(shape) - 1)
    return (lane >= HEAD_DIM) if half else (lane < HEAD_DIM)


def _stack_queries(q_ref, h0, per_slab, group):
    scale = HEAD_DIM ** -0.5
    blocks = []
    for qs in range(h0 // 2, (h0 + per_slab) // 2):
        q = (q_ref[:, qs * LANES:(qs + 1) * LANES].astype(F32) * scale).astype(BF)
        for q_half in range(2):
            k_half = ((2 * qs + q_half) // group) % 2
            qh = q if q_half == k_half else pltpu.roll(q, HEAD_DIM, axis=1)
            blocks.append(jnp.where(_half_mask(qh.shape, k_half), qh, jnp.zeros_like(qh)))
    return jnp.concatenate(blocks, axis=0)


def _attend_cols(q_ref, k_of, v_of, bias_of, sink_of, o_ref, *, n_heads, group):
    tq = q_ref.shape[0]
    per_slab = 2 * group
    n_slabs = n_heads // per_slab

    def scores(ks):
        qst = _stack_queries(q_ref, ks * per_slab, per_slab, group)
        s = lax.dot_general(k_of(ks), qst, (((1,), (1,)), ((), ())), preferred_element_type=F32)
        return bias_of(ks, s)

    s_next = scores(0)
    for ks in range(n_slabs):
        h0 = ks * per_slab
        s = s_next
        if ks + 1 < n_slabs:
            s_next = scores(ks + 1)
        m = jnp.max(s, axis=0, keepdims=True)
        if sink_of is not None:
            sink = jnp.concatenate([jnp.full((1, tq), sink_of(h), F32)
                                    for h in range(h0, h0 + per_slab)], axis=1)
            m = jnp.maximum(m, sink)
        p = jnp.exp(s - m)
        den = jnp.sum(p, axis=0, keepdims=True)
        if sink_of is not None:
            den = den + jnp.exp(sink - m)
        ot = lax.dot_general(v_of(ks), p.astype(BF), (((0,), (0,)), ((), ())),
                             preferred_element_type=F32)
        ot = ot / den
        for qs in range(h0 // 2, (h0 + per_slab) // 2):
            parts = []
            for q_half in range(2):
                h = 2 * qs + q_half
                k_half = (h // group) % 2
                parts.append(ot[k_half * HEAD_DIM:(k_half + 1) * HEAD_DIM, (h - h0) * tq:(h - h0 + 1) * tq])
            o_ref[:, qs * LANES:(qs + 1) * LANES] = jnp.concatenate(parts, axis=0).T.astype(o_ref.dtype)


def _attend(q_ref, k_of, v_of, bias_of, sink_of, o_ref, *, n_heads, group):
    tq = q_ref.shape[0]
    per_slab = 2 * group
    contract_last = (((1,), (1,)), ((), ()))
    for ks in range(n_heads // per_slab):
        h0 = ks * per_slab
        qst = _stack_queries(q_ref, h0, per_slab, group)
        scores = [jnp.dot(qst, kp, preferred_element_type=F32) if transposed else
                  lax.dot_general(qst, kp, contract_last, preferred_element_type=F32)
                  for kp, transposed in k_of(ks)]
        scores = [s.reshape(per_slab, tq, s.shape[1]) + b
                  for s, b in zip(scores, bias_of(h0, per_slab))]
        m = functools.reduce(jnp.maximum, [jnp.max(s, axis=-1, keepdims=True) for s in scores])
        if sink_of is not None:
            sink = jnp.concatenate([jnp.full((1, tq, 1), sink_of(h), F32)
                                    for h in range(h0, h0 + per_slab)], axis=0)
            m = jnp.maximum(m, sink)
        ps = [jnp.exp(s - m) for s in scores]
        den = functools.reduce(lambda a, b: a + b, [jnp.sum(p, axis=-1, keepdims=True) for p in ps])
        if sink_of is not None:
            den = den + jnp.exp(sink - m)
        pbs = [p.reshape(per_slab * tq, p.shape[2]).astype(BF) for p in ps]
        o = functools.reduce(
            lambda a, b: a + b,
            [lax.dot_general(pb, vp, contract_last, preferred_element_type=F32) if transposed else
             jnp.dot(pb, vp, preferred_element_type=F32)
             for pb, (vp, transposed) in zip(pbs, v_of(ks))])
        o = o.reshape(per_slab, tq, LANES) / den
        for qs in range(h0 // 2, (h0 + per_slab) // 2):
            halves = []
            for q_half in range(2):
                h = 2 * qs + q_half
                oh = o[h - h0]
                if (h // group) % 2 != q_half:
                    oh = pltpu.roll(oh, HEAD_DIM, axis=1)
                halves.append(oh)
            out = jnp.where(_half_mask(halves[0].shape, 0), halves[0], halves[1])
            o_ref[:, qs * LANES:(qs + 1) * LANES] = out.astype(o_ref.dtype)


def _attn_prompt_kernel(*refs, nkb, n_heads, group, has_sink, layer):
    q_ref = refs[0]
    k_refs = refs[1:1 + nkb]
    v_refs = refs[1 + nkb:1 + 2 * nkb]
    bias_ref = refs[1 + 2 * nkb]
    sink_ref = refs[2 + 2 * nkb] if has_sink else None
    o_ref = refs[-1]
    tq = q_ref.shape[0]
    first_key = (nkb - 1 - pl.program_id(0)) * tq

    def k_of(s):
        return jnp.concatenate([r[:, s * LANES:(s + 1) * LANES] for r in k_refs], axis=0)

    def v_of(s):
        return jnp.concatenate([r[:, s * LANES:(s + 1) * LANES] for r in v_refs], axis=0)

    def bias_of(ks, s):
        b = bias_ref[ks] if bias_ref.shape[0] > 1 else bias_ref[0]
        key = lax.broadcasted_iota(jnp.int32, s.shape, 0)
        return jnp.where(key < first_key, NEG_INF, s + b)

    sink_of = (lambda h: sink_ref[layer, h]) if has_sink else None
    _attend_cols(q_ref, k_of, v_of, bias_of, sink_of, o_ref, n_heads=n_heads, group=group)


def _attn_prompt(q_arr, q_blk, k_arr, k_blk, v_arr, v_blk, bias, sinks, l, *, tq, nkb, n_heads, group, name):
    m = q_arr.shape[0]
    qw = n_heads * HEAD_DIM
    kw = (n_heads // group) * HEAD_DIM
    in_specs = [pl.BlockSpec((tq, qw), lambda i: (i, q_blk))]
    args = [q_arr]
    for arr, blk in ((k_arr, k_blk), (v_arr, v_blk)):
        for b in range(nkb):
            in_specs.append(pl.BlockSpec(
                (tq, kw), lambda i, b=b, blk=blk: (jnp.maximum(i - (nkb - 1) + b, 0), blk)))
            args.append(arr)
    in_specs.append(pl.BlockSpec((None,) + bias.shape[1:], lambda i: (l, 0, 0, 0)))
    args.append(bias)
    if sinks is not None:
        in_specs.append(pl.BlockSpec(memory_space=pltpu.SMEM))
        args.append(sinks)
    return pl.pallas_call(
        functools.partial(_attn_prompt_kernel, nkb=nkb, n_heads=n_heads, group=group,
                          has_sink=sinks is not None, layer=l),
        grid=(m // tq,), in_specs=in_specs,
        out_specs=pl.BlockSpec((tq, qw), lambda i: (i, 0)),
        out_shape=jax.ShapeDtypeStruct((m, qw), BF),
        compiler_params=_params("parallel"),
        name=name,
    )(*args)


def _rel_bias(table, n_q, n_k, offset):
    d_min, d_max = offset - (n_k - 1), offset + n_q - 1
    lo, hi = max(d_min, -MAX_REL), min(d_max, MAX_REL)
    lead = table.shape[:-1]
    ext = jnp.concatenate(
        [jnp.broadcast_to(table[..., :1], lead + (lo - d_min + 1,)),
         table[..., lo + MAX_REL:hi + MAX_REL + 1],
         jnp.broadcast_to(table[..., -1:], lead + (d_max - hi,))], axis=-1)
    length = n_q + n_k
    rev = ext[..., ::-1]
    flat = jnp.tile(rev, (1,) * len(lead) + (n_q,))[..., :n_q * (length - 1)]
    toep = flat.reshape(lead + (n_q, length - 1))
    return toep[..., n_q - 1:n_q - 1 + n_k]


def _band_mask(tq, nkb, n_prev):
    r = np.arange(tq)[:, None]
    s = np.arange(nkb * tq)[None, :]
    q_chunk = (nkb - 1) * (tq // CHUNK) + r // CHUNK
    k_chunk = s // CHUNK
    return (k_chunk <= q_chunk) & (k_chunk >= q_chunk - n_prev)


def _attn_sample_kernel(*refs, n_heads, group, has_bias, has_sink, layer):
    q_ref, kn_ref, vn_ref, kc_ref, vc_ref = refs[:5]
    rest = list(refs[5:-1])
    bias_ref = rest.pop(0) if has_bias else None
    sink_ref = rest.pop(0) if has_sink else None
    o_ref = refs[-1]
    rows = kc_ref.shape[2]

    def pieces(c_ref, n_ref, s):
        cached = jnp.concatenate([c_ref[2 * s], c_ref[2 * s + 1]], axis=0).astype(BF)
        return [(cached, True), (n_ref[:, s * LANES:(s + 1) * LANES], False)]

    def k_of(s):
        return pieces(kc_ref, kn_ref, s)

    def v_of(s):
        return pieces(vc_ref, vn_ref, s)

    def bias_of(h0, nh):
        if not has_bias:
            return [0.0, 0.0]
        return [bias_ref[h0:h0 + nh, :, :rows], bias_ref[h0:h0 + nh, :, rows:]]

    sink_of = (lambda h: sink_ref[layer, h]) if has_sink else None
    _attend(q_ref, k_of, v_of, bias_of, sink_of, o_ref, n_heads=n_heads, group=group)


def _attn_sample(q_arr, q_blk, k_arr, k_blk, v_arr, v_blk, k_cache, v_cache, bias, sinks, l, *,
                 t, n_heads, group, name):
    _, bsz, kv_heads, hd, rows = k_cache.shape
    qw, kw = n_heads * HEAD_DIM, kv_heads * hd
    in_specs = [pl.BlockSpec((t, qw), lambda i: (i, q_blk)),
                pl.BlockSpec((t, kw), lambda i: (i, k_blk)),
                pl.BlockSpec((t, kw), lambda i: (i, v_blk)),
                pl.BlockSpec((None, None, kv_heads, hd, rows), lambda i: (l, i, 0, 0, 0)),
                pl.BlockSpec((None, None, kv_heads, hd, rows), lambda i: (l, i, 0, 0, 0))]
    args = [q_arr, k_arr, v_arr, k_cache, v_cache]
    if bias is not None:
        in_specs.append(pl.BlockSpec((None,) + bias.shape[1:], lambda i: (l, 0, 0, 0)))
        args.append(bias)
    if sinks is not None:
        in_specs.append(pl.BlockSpec(memory_space=pltpu.SMEM))
        args.append(sinks)
    return pl.pallas_call(
        functools.partial(_attn_sample_kernel, n_heads=n_heads, group=group,
                          has_bias=bias is not None, has_sink=sinks is not None, layer=l),
        grid=(bsz,), in_specs=in_specs,
        out_specs=pl.BlockSpec((t, qw), lambda i: (i, 0)),
        out_shape=jax.ShapeDtypeStruct((bsz * t, qw), BF),
        compiler_params=_params("parallel"),
        name=name,
    )(*args)


def _merge_kernel(*refs, n_branch):
    br = refs[:n_branch]
    gt = refs[n_branch:2 * n_branch]
    wb = refs[2 * n_branch:3 * n_branch]
    o_ref = refs[-1]
    acc = None
    for b, g, w in zip(br, gt, wb):
        y = g[...].astype(F32) * jnp.dot(b[...], w[...], preferred_element_type=F32)
        acc = y if acc is None else acc + y
    o_ref[...] = acc.astype(o_ref.dtype)


def _merge(branches, gates, w_branch, l, *, name, bm=512, bn=1024):
    n_branch = len(branches)
    m, kdim = branches[0].shape
    d = w_branch.shape[3]
    bm, bn = min(bm, m), min(bn, d)
    nj = d // bn
    in_specs = [pl.BlockSpec((bm, kdim), lambda j, i: (i, 0)) for _ in range(n_branch)]
    in_specs += [pl.BlockSpec((bm, bn), lambda j, i, b=b: (i, b * nj + j)) for b in range(n_branch)]
    in_specs += [pl.BlockSpec((None, None, kdim, bn), lambda j, i, b=b: (l, b, 0, j)) for b in range(n_branch)]
    return pl.pallas_call(
        functools.partial(_merge_kernel, n_branch=n_branch),
        grid=(nj, m // bm), in_specs=in_specs,
        out_specs=pl.BlockSpec((bm, bn), lambda j, i: (i, j)),
        out_shape=jax.ShapeDtypeStruct((m, d), BF),
        compiler_params=_params("parallel", "arbitrary"),
        name=name,
    )(*branches, *([gates] * n_branch), *([w_branch] * n_branch))


def _run_path(tag, x3, mods, caches, pos, w, p):
    bsz, t, d = x3.shape
    m = bsz * t
    depth = w["in"].shape[0]
    br = d // 2
    n_heads = br // HEAD_DIM
    kvw = SWA_KV_HEADS * HEAD_DIM
    prompt = caches is None
    alpha = (2 * depth) ** 0.25
    col = w["cols"]

    def rowwise(a):
        return a if bsz == 1 else jnp.repeat(a, t, axis=0)

    def vec(a):
        return a[:, None, :]

    tables = _rope_tables(jnp.tile(pos, bsz))
    blk = min(t, GMLP_BLOCK)
    rep = GMLP_BLOCK // blk
    ws = jnp.tile(p["w_spatial"][:, :, :blk, :blk], (1, 1, rep, rep))
    bmap = jnp.repeat(jnp.swapaxes(jnp.tile(p["b_spatial"][:, :, :blk], (1, 1, rep)), 1, 2),
                      br // GMLP_GROUPS, axis=2)
    sinks = p["swa_sinks"]
    conv_w8 = jnp.broadcast_to(p["conv_w"][:, :, None, :], (depth, CONV_WIDTH, SUBLANES, br))
    if prompt:
        tq_c, nkb_c, tq_d, nkb_d = 128, 2, 128, 5
        per_c = 2 * n_heads // SWA_KV_HEADS
        bias_c = jnp.where(_band_mask(tq_c, nkb_c, SWA_PREV), 0.0, NEG_INF).astype(F32).T
        bias_c = jnp.broadcast_to(jnp.tile(bias_c, (1, per_c)), (depth, 1, nkb_c * tq_c, per_c * tq_c))
        bias_d = jnp.where(_band_mask(tq_d, nkb_d, BAND_PREV),
                           _rel_bias(p["band_rel_bias"], tq_d, nkb_d * tq_d, (nkb_d - 1) * tq_d).astype(F32),
                           NEG_INF)
        bias_d = bias_d.reshape(depth, n_heads // 2, 2 * tq_d, nkb_d * tq_d).swapaxes(2, 3)
    else:
        rows_d = caches[3].shape[2]
        bias_d = _rel_bias(p["band_rel_bias"], t, rows_d + t, rows_d).astype(F32)
        hist = jnp.pad(caches[0], ((0, 0), (0, 0), (CONV_HALO - (CONV_WIDTH - 1), 0), (0, 0)))
        hist = hist.reshape(depth, bsz * CONV_HALO, br)
        ck, cv, bk, bv = (jnp.transpose(c, (0, 1, 3, 4, 2)) for c in caches[1:])

    x = x3.reshape(m, d)
    h = _modulate(x, rowwise(mods[0][1]), rowwise(mods[0][0]), name=f"{tag}_mod0")
    states = []
    for l in range(depth):
        _, _, g1, sh2, sc2, g2 = mods[l]
        wide = dict(bm=2048, bn=512)
        ug = _mm(h, w["in"], l, col["gmlp"], 2 * br, act="gelu", name=f"{tag}_proj_gmlp", **wide)
        glu = _mm(h, w["in"], l, col["glu"], br, act="glu", bn=512, name=f"{tag}_proj_glu")
        swa = _mm(h, w["in"], l, col["swa"], br + 2 * kvw, name=f"{tag}_proj_swa", **wide)
        band = _mm(h, w["in"], l, col["band"], 3 * br, name=f"{tag}_proj_band", **wide)
        gates = _mm(h, w["in"], l, col["gate"], 4 * d, act="sigmoid", name=f"{tag}_proj_gate", **wide)
        res = _gmlp(ug, ws, bmap, vec(p["gmlp_ln_g"]), vec(p["gmlp_ln_b"]), l,
                    blk=blk, emit_vn=not prompt, name=f"{tag}_gmlp")
        out_a = res[0]
        conv_args = (conv_w8, vec(p["conv_b"]), vec(p["conv_ln_g"]), vec(p["conv_ln_b"]), l)
        if prompt:
            out_b = _conv(glu, glu, *conv_args, rows=256, prev_is_glu=True, name=f"{tag}_conv")
        else:
            out_b = _conv(glu, hist[l], *conv_args, rows=t, prev_is_glu=False, name=f"{tag}_conv")
        new_conv = glu.reshape(bsz, t, br)[:, t - (CONV_WIDTH - 1):].astype(F32)
        qk = _rope(swa, br + kvw, tables, name=f"{tag}_rope")
        new = lambda a, heads: a.astype(F32).reshape(bsz, -1, heads, HEAD_DIM)
        if prompt:
            out_c = _attn_prompt(qk, 0, qk, br // kvw, swa, (br + kvw) // kvw, bias_c, sinks, l,
                                 tq=tq_c, nkb=nkb_c, n_heads=n_heads, group=n_heads // SWA_KV_HEADS,
                                 name=f"{tag}_swa")
            out_d = _attn_prompt(band, 0, band, 1, band, 2, bias_d, None, l,
                                 tq=tq_d, nkb=nkb_d, n_heads=n_heads, group=1, name=f"{tag}_band")
            swa_k = new(qk[m - SWA_PREV * CHUNK:, br:], SWA_KV_HEADS)
            swa_v = new(swa[m - SWA_PREV * CHUNK:, br + kvw:], SWA_KV_HEADS)
            band_k = new(band[m - BAND_PREV * CHUNK:, br:2 * br], n_heads)
            band_v = new(band[m - BAND_PREV * CHUNK:, 2 * br:], n_heads)
        else:
            out_c = _attn_sample(qk, 0, qk, br // kvw, swa, (br + kvw) // kvw, ck, cv, None, sinks, l,
                                 t=t, n_heads=n_heads, group=n_heads // SWA_KV_HEADS, name=f"{tag}_swa")
            out_d = _attn_sample(band, 0, band, 1, band, 2, bk, bv, bias_d, None, l,
                                 t=t, n_heads=n_heads, group=1, name=f"{tag}_band")
            swa_k = new(qk[:, br:], SWA_KV_HEADS)
            swa_v = new(swa[:, br + kvw:], SWA_KV_HEADS)
            band_k = new(band[:, br:2 * br], n_heads)
            band_v = new(band[:, 2 * br:], n_heads)
        merged = _merge([out_a, out_b, out_c, out_d], gates, w["branch"], l, name=f"{tag}_merge")
        x, h2 = _mm_ln(merged, w["out"], l, x, g1, vec(p["ln1_g"]), vec(p["ln1_b"]), sc2, sh2, seg=t,
                       alpha=alpha, name=f"{tag}_out_ln")
        hid = _mm(h2, w["ff1"], l, 0, w["ff1"].shape[2], act="relu2", name=f"{tag}_ff1", **wide)
        nxt = mods[min(l + 1, depth - 1)]
        x, h = _mm_ln(hid, w["ff2"], l, x, g2, vec(p["ln2_g"]), vec(p["ln2_b"]),
                      nxt[1], nxt[0], seg=t, alpha=alpha, name=f"{tag}_ff2_ln")
        st = [new_conv, swa_k, swa_v, band_k, band_v]
        if not prompt:
            st.append(res[1].reshape(bsz, t, br))
        states.append(st)
    stacked = [jnp.stack(s, axis=0) for s in zip(*states)]
    if not prompt:
        for i in range(1, 5):
            stacked[i] = jnp.concatenate([caches[i][:, :, t:], stacked[i]], axis=2)
    return x.reshape(bsz, t, d), stacked


def _mxu_weights(w_ada, w_in, w_branch, w_out, w_ff1, w_ff2):
    br = w_branch.shape[2]
    kvw = SWA_KV_HEADS * HEAD_DIM
    cols, o = {}, 0
    for name, size in (("gmlp", 2 * br), ("glu", 2 * br), ("swa", br + 2 * kvw), ("band", 3 * br)):
        cols[name] = o
        o += size
    cols["gate"] = o
    cast = lambda a: a.astype(BF)
    return {"in": w_in, "cols": cols, "ada": w_ada, "branch": cast(w_branch),
            "out": cast(w_out), "ff1": w_ff1, "ff2": cast(w_ff2)}


def kernel(x_prompt, x_sample, state_conv, cache_swa_k, cache_swa_v, cache_band_k, cache_band_v, c_prompt, c_sample, w_ada, b_ada, w_in, gmlp_ln_g, gmlp_ln_b, w_spatial, b_spatial, conv_w, conv_b, conv_ln_g, conv_ln_b, swa_sinks, band_rel_bias, w_branch, w_out, ln1_g, ln1_b, w_ff1, w_ff2, ln2_g, ln2_b):
    depth = w_in.shape[0]
    d = x_prompt.shape[2]
    p = dict(gmlp_ln_g=gmlp_ln_g, gmlp_ln_b=gmlp_ln_b, w_spatial=w_spatial, b_spatial=b_spatial,
             conv_w=conv_w, conv_b=conv_b, conv_ln_g=conv_ln_g, conv_ln_b=conv_ln_b,
             swa_sinks=swa_sinks, band_rel_bias=band_rel_bias, ln1_g=ln1_g, ln1_b=ln1_b,
             ln2_g=ln2_g, ln2_b=ln2_b)
    w = _mxu_weights(w_ada, w_in, w_branch, w_out, w_ff1, w_ff2)
    nb_p, nb_s = c_prompt.shape[0], c_sample.shape[0]
    pad = (-(nb_p + nb_s)) % SUBLANES
    c_all = jnp.concatenate([c_prompt, c_sample, jnp.zeros((pad, d), F32)], axis=0)
    mods_p, mods_s = [], []
    for l in range(depth):
        mod = _mm(c_all, w["ada"], l, 0, w_ada.shape[2], pre="silu", bias=b_ada[:, None, :],
                  out_dtype=F32, name="adaln")
        parts = jnp.split(mod, 6, axis=-1)
        mods_p.append([a[:nb_p] for a in parts])
        mods_s.append([a[nb_p:nb_p + nb_s] for a in parts])
    pos_p = jnp.arange(x_prompt.shape[1], dtype=jnp.int32)
    pos_s = PAST_LEN + jnp.arange(x_sample.shape[1], dtype=jnp.int32)
    y_p, st_p = _run_path("p", x_prompt, mods_p, None, pos_p, w, p)
    y_s, st_s = _run_path("s", x_sample, mods_s,
                          (state_conv, cache_swa_k, cache_swa_v, cache_band_k, cache_band_v),
                          pos_s, w, p)
    conv_p, swa_k_p, swa_v_p, band_k_p, band_v_p = st_p
    conv_s, swa_k_s, swa_v_s, band_k_s, band_v_s, gmlp_v_s = st_s
    return (y_p, y_s, conv_p, conv_s, swa_k_p, swa_v_p, swa_k_s, swa_v_s,
            band_k_p, band_v_p, band_k_s, band_v_s, gmlp_v_s)
```

```python
import functools

import jax
import jax.numpy as jnp
import numpy as np
from jax import lax
from jax.experimental import pallas as pl
from jax.experimental.pallas import tpu as pltpu

BF = jnp.bfloat16
F32 = jnp.float32

CHUNK = 64
HEAD_DIM = 64
GMLP_BLOCK = 128
GMLP_GROUPS = 8
CONV_WIDTH = 31
SWA_KV_HEADS = 4
SWA_PREV = 2
BAND_PREV = 8
MAX_REL = 256
PAST_LEN = 1024
ROPE_THETA = 500000.0
ROT_DIM = HEAD_DIM // 4
LN_EPS = 1e-5
NEG_INF = -1e30

LANES = 128
SUBLANES = 8
CONV_HALO = 32
VMEM_LIMIT = 56 * 1024 * 1024


def _params(*sem):
    return pltpu.CompilerParams(dimension_semantics=sem, vmem_limit_bytes=VMEM_LIMIT)


def _ln(x, g, b):
    mu = jnp.mean(x, axis=-1, keepdims=True)
    xc = x - mu
    var = jnp.mean(xc * xc, axis=-1, keepdims=True)
    return xc * lax.rsqrt(var + LN_EPS) * g + b


def _sigmoid(x):
    return 0.5 * jnp.tanh(0.5 * x) + 0.5


def _mm_kernel(*refs, act, pre, has_bias):
    a_ref, o_ref = refs[0], refs[-1]
    a = a_ref[...]
    if pre == "silu":
        a = a * _sigmoid(a)
    a = a.astype(BF)
    acc = jnp.dot(a, refs[1][...].astype(BF), preferred_element_type=F32)
    if has_bias:
        acc = acc + refs[2][...]
    if act == "gelu":
        acc = jax.nn.gelu(acc)
    elif act == "sigmoid":
        acc = _sigmoid(acc)
    elif act == "relu2":
        acc = jnp.square(jnp.maximum(acc, 0.0))
    elif act == "glu":
        acc = acc * _sigmoid(jnp.dot(a, refs[2][...].astype(BF), preferred_element_type=F32))
    o_ref[...] = acc.astype(o_ref.dtype)


def _mm(a, w, l, col0, n, *, name, act=None, pre=None, bias=None, out_dtype=BF, bm=1024, bn=1024):
    m, k = a.shape
    bm, bn = min(bm, m), min(bn, n)
    assert m % bm == 0 and n % bn == 0 and col0 % bn == 0, (m, n, bm, bn, col0)
    c0 = col0 // bn
    in_specs = [pl.BlockSpec((bm, k), lambda i, j: (i, 0)),
                pl.BlockSpec((None, k, bn), lambda i, j: (l, 0, c0 + j))]
    args = [a, w]
    if act == "glu":
        g0 = (col0 + n) // bn
        in_specs.append(pl.BlockSpec((None, k, bn), lambda i, j: (l, 0, g0 + j)))
        args.append(w)
    if bias is not None:
        in_specs.append(pl.BlockSpec((None, 1, bn), lambda i, j: (l, 0, j)))
        args.append(bias)
    return pl.pallas_call(
        functools.partial(_mm_kernel, act=act, pre=pre, has_bias=bias is not None),
        grid=(m // bm, n // bn),
        in_specs=in_specs,
        out_specs=pl.BlockSpec((bm, bn), lambda i, j: (i, j)),
        out_shape=jax.ShapeDtypeStruct((m, n), out_dtype),
        compiler_params=_params("parallel", "arbitrary"),
        name=name,
    )(*args)


LN_ROWS = 128


def _mm_ln_kernel(a_ref, w_ref, x_ref, g_ref, lng_ref, lnb_ref, sc_ref, sh_ref,
                  xo_ref, ho_ref, acc0_ref, acc1_ref, *, alpha, nk, n_tiles, seg):
    i, k = pl.program_id(0), pl.program_id(1)
    step = min(LN_ROWS, seg)

    def mod(ref, r):
        if ref.shape[0] == 1:
            return ref[...]
        b = (r * step) // seg
        return ref[b:b + 1, :]

    def epilogue(done_ref):
        for r in range(xo_ref.shape[0] // step):
            rows = slice(r * step, (r + 1) * step)
            y = alpha * x_ref[rows, :] + (1.0 + mod(g_ref, r)) * done_ref[rows, :]
            xn = _ln(y, lng_ref[...], lnb_ref[...])
            xo_ref[rows, :] = xn
            ho_ref[rows, :] = (xn * (1.0 + mod(sc_ref, r)) + mod(sh_ref, r)).astype(ho_ref.dtype)

    def matmul(acc_ref, first):
        part = jnp.dot(a_ref[...], w_ref[...], preferred_element_type=F32)
        if first:
            acc_ref[...] = part
        else:
            acc_ref[...] += part

    for parity, (cur_ref, done_ref) in enumerate(((acc0_ref, acc1_ref), (acc1_ref, acc0_ref))):
        mine = (i % 2) == parity

        @pl.when(mine & (i > 0) & (i < n_tiles) & (k == 0))
        def _():
            matmul(cur_ref, True)
            epilogue(done_ref)

        @pl.when(mine & (i < n_tiles) & (k > 0))
        def _():
            matmul(cur_ref, False)

        if parity == 0:
            @pl.when((i == 0) & (k == 0))
            def _():
                matmul(cur_ref, True)

        if n_tiles % 2 == parity:
            @pl.when((i == n_tiles) & (k == 0))
            def _():
                epilogue(done_ref)


def _mm_ln(a, w, l, x, gate, ln_g, ln_b, scale, shift, *, alpha, seg, name, bm=512, bk=2048):
    m, kdim = a.shape
    d = w.shape[2]
    bm, bk = min(bm, m), min(bk, kdim)
    assert m % bm == 0 and kdim % bk == 0 and bm % min(LN_ROWS, seg) == 0
    n_tiles, nk = m // bm, kdim // bk
    last = n_tiles - 1
    k_of = lambda i, k: jnp.where(i > last, nk - 1, k)
    done = lambda i, k: (jnp.maximum(i - 1, 0), 0)
    const = pl.BlockSpec((None, 1, d), lambda i, k: (l, 0, 0))
    rows = pl.BlockSpec((bm, d), done)

    def mod_spec(arr):
        if arr.shape[0] == 1:
            return pl.BlockSpec((1, d), lambda i, k: (0, 0))
        assert arr.shape[0] * seg == m and bm % seg == 0
        return pl.BlockSpec((bm // seg, d), done)

    return pl.pallas_call(
        functools.partial(_mm_ln_kernel, alpha=alpha, nk=nk, n_tiles=n_tiles, seg=seg),
        grid=(n_tiles + 1, nk),
        in_specs=[pl.BlockSpec((bm, bk), lambda i, k: (jnp.minimum(i, last), k_of(i, k))),
                  pl.BlockSpec((None, bk, d), lambda i, k: (l, k_of(i, k), 0)),
                  rows, mod_spec(gate), const, const, mod_spec(scale), mod_spec(shift)],
        out_specs=[rows, rows],
        out_shape=[jax.ShapeDtypeStruct((m, d), F32), jax.ShapeDtypeStruct((m, d), BF)],
        scratch_shapes=[pltpu.VMEM((bm, d), F32), pltpu.VMEM((bm, d), F32)],
        compiler_params=_params("arbitrary", "arbitrary"),
        name=name,
    )(a, w, x, gate, ln_g, ln_b, scale, shift)


def _mod_kernel(x_ref, sc_ref, sh_ref, o_ref):
    o_ref[...] = (x_ref[...] * (1.0 + sc_ref[...]) + sh_ref[...]).astype(o_ref.dtype)


def _modulate(x, scale, shift, *, name, bm=512):
    m, d = x.shape
    bm = min(bm, m)
    rows = pl.BlockSpec((bm, d), lambda i: (i, 0))

    def spec(arr):
        if arr.shape[0] == 1:
            return pl.BlockSpec((1, d), lambda i: (0, 0))
        return rows

    return pl.pallas_call(
        _mod_kernel, grid=(m // bm,),
        in_specs=[rows, spec(scale), spec(shift)],
        out_specs=rows,
        out_shape=jax.ShapeDtypeStruct((m, d), BF),
        compiler_params=_params("parallel"),
        name=name,
    )(x, scale, shift)


def _rope_kernel(x_ref, c_ref, sa_ref, sb_ref, o_ref):
    half = ROT_DIM // 2
    c, sa, sb = c_ref[...], sa_ref[...], sb_ref[...]
    for s in range(x_ref.shape[1] // LANES):
        x = x_ref[:, s * LANES:(s + 1) * LANES].astype(F32)
        up = pltpu.roll(x, LANES - half, axis=1)
        dn = pltpu.roll(x, half, axis=1)
        o_ref[:, s * LANES:(s + 1) * LANES] = (x * c + up * sa + dn * sb).astype(o_ref.dtype)


def _rope(slab, width, tables, *, name, bm=512):
    m = slab.shape[0]
    bm = min(bm, m)
    tab = pl.BlockSpec((bm, LANES), lambda i: (i, 0))
    return pl.pallas_call(
        _rope_kernel, grid=(m // bm,),
        in_specs=[pl.BlockSpec((bm, width), lambda i: (i, 0)), tab, tab, tab],
        out_specs=pl.BlockSpec((bm, width), lambda i: (i, 0)),
        out_shape=jax.ShapeDtypeStruct((m, width), BF),
        compiler_params=_params("parallel"),
        name=name,
    )(slab, *tables)


def _rope_tables(pos):
    half = ROT_DIM // 2
    inv_freq = jnp.power(jnp.float32(ROPE_THETA), -jnp.arange(half, dtype=F32) / half)
    ang = pos.astype(F32)[:, None] * inv_freq[None, :]
    cos, sin = jnp.cos(ang), jnp.sin(ang)
    t = pos.shape[0]
    ones = jnp.ones((t, HEAD_DIM - ROT_DIM), F32)
    zeros = jnp.zeros((t, HEAD_DIM - ROT_DIM), F32)
    zh = jnp.zeros((t, half), F32)
    c = jnp.concatenate([cos, cos, ones], axis=1)
    sa = jnp.concatenate([-sin, zh, zeros], axis=1)
    sb = jnp.concatenate([zh, sin, zeros], axis=1)
    rep = LANES // HEAD_DIM
    return tuple(jnp.tile(a, (1, rep)) for a in (c, sa, sb))


def _gmlp_kernel(ug_ref, ws_ref, bmap_ref, lng_ref, lnb_ref, o_ref, *vn_refs, blk):
    width = o_ref.shape[1]
    gc = width // GMLP_GROUPS
    t = lax.broadcasted_iota(jnp.int32, (GMLP_BLOCK, GMLP_BLOCK), 0)
    s = lax.broadcasted_iota(jnp.int32, (GMLP_BLOCK, GMLP_BLOCK), 1)
    mask = (s <= t) & (s >= (t // blk) * blk)
    ws = [jnp.where(mask, ws_ref[g], 0.0).astype(BF) for g in range(GMLP_GROUPS)]
    bmap = bmap_ref[...]
    for r in range(o_ref.shape[0] // GMLP_BLOCK):
        rows = slice(r * GMLP_BLOCK, (r + 1) * GMLP_BLOCK)
        vn = _ln(ug_ref[rows, width:].astype(F32), lng_ref[...], lnb_ref[...])
        if vn_refs:
            vn_refs[0][rows, :] = vn
        vnb = vn.astype(BF)
        for g in range(GMLP_GROUPS):
            cols = slice(g * gc, (g + 1) * gc)
            mixed = jnp.dot(ws[g], vnb[:, cols], preferred_element_type=F32)
            u = ug_ref[rows, cols].astype(F32)
            o_ref[rows, cols] = (u * (mixed + bmap[:, cols])).astype(o_ref.dtype)


def _gmlp(ug, ws, bmap, ln_g, ln_b, l, *, blk, emit_vn, name, bm=512):
    m, two_w = ug.shape
    width = two_w // 2
    bm = min(bm, m)
    vec = pl.BlockSpec((None, 1, width), lambda i: (l, 0, 0))
    out_specs = [pl.BlockSpec((bm, width), lambda i: (i, 0))]
    out_shape = [jax.ShapeDtypeStruct((m, width), BF)]
    if emit_vn:
        out_specs.append(pl.BlockSpec((bm, width), lambda i: (i, 0)))
        out_shape.append(jax.ShapeDtypeStruct((m, width), F32))
    return pl.pallas_call(
        functools.partial(_gmlp_kernel, blk=blk), grid=(m // bm,),
        in_specs=[pl.BlockSpec((bm, two_w), lambda i: (i, 0)),
                  pl.BlockSpec((None, GMLP_GROUPS, GMLP_BLOCK, GMLP_BLOCK), lambda i: (l, 0, 0, 0)),
                  pl.BlockSpec((None, GMLP_BLOCK, width), lambda i: (l, 0, 0)),
                  vec, vec],
        out_specs=out_specs, out_shape=out_shape,
        compiler_params=_params("parallel"),
        name=name,
    )(ug, ws, bmap, ln_g, ln_b)


CONV_ROWS = 32
CONV_COPY_ROWS = 56


def _conv_body(cur_ref, prev_ref, w_ref, cb_ref, lng_ref, lnb_ref, o_ref, xs_ref, no_history):
    rows = cur_ref.shape[0]
    prev = prev_ref[...].astype(F32)
    if no_history is not None:
        prev = jnp.where(no_history, 0.0, prev)
    xs_ref[0, 0:CONV_HALO, :] = prev
    xs_ref[0, CONV_HALO:CONV_HALO + rows, :] = cur_ref[...].astype(F32)
    n_shift = CONV_HALO + rows - SUBLANES
    piece = max(p for p in range(SUBLANES, CONV_COPY_ROWS + 1, SUBLANES) if n_shift % p == 0)
    for b in range(1, SUBLANES):
        for r0 in range(0, n_shift, piece):
            xs_ref[b, r0:r0 + piece, :] = xs_ref[0, r0 + b:r0 + b + piece, :]
    first = CONV_HALO - (CONV_WIDTH - 1)
    groups = CONV_ROWS // SUBLANES
    for c in range(rows // CONV_ROWS):
        accs = [jnp.zeros((SUBLANES, cur_ref.shape[1]), F32) for _ in range(groups)]
        for j in range(CONV_WIDTH):
            b = (first + j) % SUBLANES
            r0 = c * CONV_ROWS + first + j - b
            w = w_ref[j]
            for g in range(groups):
                rg = r0 + g * SUBLANES
                accs[g] = accs[g] + w * xs_ref[b, rg:rg + SUBLANES, :]
        acc = jnp.concatenate(accs, axis=0)
        y = _ln(acc + cb_ref[...], lng_ref[...], lnb_ref[...])
        o_ref[c * CONV_ROWS:(c + 1) * CONV_ROWS, :] = (y * _sigmoid(y)).astype(o_ref.dtype)


def _conv_kernel(*refs, zero_first):
    _conv_body(*refs, (pl.program_id(0) == 0) if zero_first else None)


def _conv(glu, prev, conv_w, conv_b, ln_g, ln_b, l, *, rows, prev_is_glu, name):
    m, width = glu.shape
    rows = min(rows, m)
    step = rows // CONV_HALO
    if prev_is_glu:
        prev_spec = pl.BlockSpec((CONV_HALO, width), lambda i: (jnp.maximum(i * step - 1, 0), 0))
    else:
        prev_spec = pl.BlockSpec((CONV_HALO, width), lambda i: (i, 0))
    vec = pl.BlockSpec((None, 1, width), lambda i: (l, 0, 0))
    return pl.pallas_call(
        functools.partial(_conv_kernel, zero_first=prev_is_glu), grid=(m // rows,),
        in_specs=[pl.BlockSpec((rows, width), lambda i: (i, 0)), prev_spec,
                  pl.BlockSpec((None, CONV_WIDTH, SUBLANES, width), lambda i: (l, 0, 0, 0)),
                  vec, vec, vec],
        out_specs=pl.BlockSpec((rows, width), lambda i: (i, 0)),
        out_shape=jax.ShapeDtypeStruct((m, width), BF),
        scratch_shapes=[pltpu.VMEM((SUBLANES, CONV_HALO + rows, width), F32)],
        compiler_params=_params("parallel"),
        name=name,
    )(glu, prev, conv_w, conv_b, ln_g, ln_b)


def _half_mask(shape, half):
    lane = lax.broadcasted_iota(jnp.int32, shape, len(shape) - 1)
    return (lane >= HEAD_DIM) if half else (lane < HEAD_DIM)


def _stack_queries(q_ref, h0, per_slab, group):
    scale = HEAD_DIM ** -0.5
    blocks = []
    for qs in range(h0 // 2, (h0 + per_slab) // 2):
        q = (q_ref[:, qs * LANES:(qs + 1) * LANES].astype(F32) * scale).astype(BF)
        for q_half in range(2):
            k_half = ((2 * qs + q_half) // group) % 2
            qh = q if q_half == k_half else pltpu.roll(q, HEAD_DIM, axis=1)
            blocks.append(jnp.where(_half_mask(qh.shape, k_half), qh, jnp.zeros_like(qh)))
    return jnp.concatenate(blocks, axis=0)


def _attend_cols(q_ref, k_of, v_of, bias_of, sink_of, o_ref, *, n_heads, group):
    tq = q_ref.shape[0]
    per_slab = 2 * group
    n_slabs = n_heads // per_slab

    def scores(ks):
        qst = _stack_queries(q_ref, ks * per_slab, per_slab, group)
        s = lax.dot_general(k_of(ks), qst, (((1,), (1,)), ((), ())), preferred_element_type=F32)
        return bias_of(ks, s)

    s_next = scores(0)
    for ks in range(n_slabs):
        h0 = ks * per_slab
        s = s_next
        if ks + 1 < n_slabs:
            s_next = scores(ks + 1)
        m = jnp.max(s, axis=0, keepdims=True)
        if sink_of is not None:
            sink = jnp.concatenate([jnp.full((1, tq), sink_of(h), F32)
                                    for h in range(h0, h0 + per_slab)], axis=1)
            m = jnp.maximum(m, sink)
        p = jnp.exp(s - m)
        den = jnp.sum(p, axis=0, keepdims=True)
        if sink_of is not None:
            den = den + jnp.exp(sink - m)
        ot = lax.dot_general(v_of(ks), p.astype(BF), (((0,), (0,)), ((), ())),
                             preferred_element_type=F32)
        ot = ot / den
        for qs in range(h0 // 2, (h0 + per_slab) // 2):
            parts = []
            for q_half in range(2):
                h = 2 * qs + q_half
                k_half = (h // group) % 2
                parts.append(ot[k_half * HEAD_DIM:(k_half + 1) * HEAD_DIM, (h - h0) * tq:(h - h0 + 1) * tq])
            o_ref[:, qs * LANES:(qs + 1) * LANES] = jnp.concatenate(parts, axis=0).T.astype(o_ref.dtype)


def _attend(q_ref, k_of, v_of, bias_of, sink_of, o_ref, *, n_heads, group):
    tq = q_ref.shape[0]
    per_slab = 2 * group
    contract_last = (((1,), (1,)), ((), ()))
    for ks in range(n_heads // per_slab):
        h0 = ks * per_slab
        qst = _stack_queries(q_ref, h0, per_slab, group)
        scores = [jnp.dot(qst, kp, preferred_element_type=F32) if transposed else
                  lax.dot_general(qst, kp, contract_last, preferred_element_type=F32)
                  for kp, transposed in k_of(ks)]
        scores = [s.reshape(per_slab, tq, s.shape[1]) + b
                  for s, b in zip(scores, bias_of(h0, per_slab))]
        m = functools.reduce(jnp.maximum, [jnp.max(s, axis=-1, keepdims=True) for s in scores])
        if sink_of is not None:
            sink = jnp.concatenate([jnp.full((1, tq, 1), sink_of(h), F32)
                                    for h in range(h0, h0 + per_slab)], axis=0)
            m = jnp.maximum(m, sink)
        ps = [jnp.exp(s - m) for s in scores]
        den = functools.reduce(lambda a, b: a + b, [jnp.sum(p, axis=-1, keepdims=True) for p in ps])
        if sink_of is not None:
            den = den + jnp.exp(sink - m)
        pbs = [p.reshape(per_slab * tq, p.shape[2]).astype(BF) for p in ps]
        o = functools.reduce(
            lambda a, b: a + b,
            [lax.dot_general(pb, vp, contract_last, preferred_element_type=F32) if transposed else
             jnp.dot(pb, vp, preferred_element_type=F32)
             for pb, (vp, transposed) in zip(pbs, v_of(ks))])
        o = o.reshape(per_slab, tq, LANES) / den
        for qs in range(h0 // 2, (h0 + per_slab) // 2):
            halves = []
            for q_half in range(2):
                h = 2 * qs + q_half
                oh = o[h - h0]
                if (h // group) % 2 != q_half:
                    oh = pltpu.roll(oh, HEAD_DIM, axis=1)
                halves.append(oh)
            out = jnp.where(_half_mask(halves[0].shape, 0), halves[0], halves[1])
            o_ref[:, qs * LANES:(qs + 1) * LANES] = out.astype(o_ref.dtype)


def _attn_prompt_kernel(*refs, nkb, n_heads, group, has_sink, layer):
    q_ref = refs[0]
    k_refs = refs[1:1 + nkb]
    v_refs = refs[1 + nkb:1 + 2 * nkb]
    bias_ref = refs[1 + 2 * nkb]
    sink_ref = refs[2 + 2 * nkb] if has_sink else None
    o_ref = refs[-1]
    tq = q_ref.shape[0]
    first_key = (nkb - 1 - pl.program_id(0)) * tq

    def k_of(s):
        return jnp.concatenate([r[:, s * LANES:(s + 1) * LANES] for r in k_refs], axis=0)

    def v_of(s):
        return jnp.concatenate([r[:, s * LANES:(s + 1) * LANES] for r in v_refs], axis=0)

    def bias_of(ks, s):
        return s + (bias_ref[ks] if bias_ref.shape[0] > 1 else bias_ref[0])

    def bias_at_start(ks, s):
        key = lax.broadcasted_iota(jnp.int32, s.shape, 0)
        return jnp.where(key < first_key, NEG_INF, bias_of(ks, s))

    sink_of = (lambda h: sink_ref[layer, h]) if has_sink else None

    @pl.when(first_key > 0)
    def _():
        _attend_cols(q_ref, k_of, v_of, bias_at_start, sink_of, o_ref, n_heads=n_heads, group=group)

    @pl.when(first_key <= 0)
    def _():
        _attend_cols(q_ref, k_of, v_of, bias_of, sink_of, o_ref, n_heads=n_heads, group=group)


def _attn_prompt(q_arr, q_blk, k_arr, k_blk, v_arr, v_blk, bias, sinks, l, *, tq, nkb, n_heads, group, name):
    m = q_arr.shape[0]
    qw = n_heads * HEAD_DIM
    kw = (n_heads // group) * HEAD_DIM
    in_specs = [pl.BlockSpec((tq, qw), lambda i: (i, q_blk))]
    args = [q_arr]
    for arr, blk in ((k_arr, k_blk), (v_arr, v_blk)):
        for b in range(nkb):
            in_specs.append(pl.BlockSpec(
                (tq, kw), lambda i, b=b, blk=blk: (jnp.maximum(i - (nkb - 1) + b, 0), blk)))
            args.append(arr)
    in_specs.append(pl.BlockSpec((None,) + bias.shape[1:], lambda i: (l, 0, 0, 0)))
    args.append(bias)
    if sinks is not None:
        in_specs.append(pl.BlockSpec(memory_space=pltpu.SMEM))
        args.append(sinks)
    return pl.pallas_call(
        functools.partial(_attn_prompt_kernel, nkb=nkb, n_heads=n_heads, group=group,
                          has_sink=sinks is not None, layer=l),
        grid=(m // tq,), in_specs=in_specs,
        out_specs=pl.BlockSpec((tq, qw), lambda i: (i, 0)),
        out_shape=jax.ShapeDtypeStruct((m, qw), BF),
        compiler_params=_params("parallel"),
        name=name,
    )(*args)


def _rel_bias(table, n_q, n_k, offset):
    d_min, d_max = offset - (n_k - 1), offset + n_q - 1
    lo, hi = max(d_min, -MAX_REL), min(d_max, MAX_REL)
    lead = table.shape[:-1]
    ext = jnp.concatenate(
        [jnp.broadcast_to(table[..., :1], lead + (lo - d_min + 1,)),
         table[..., lo + MAX_REL:hi + MAX_REL + 1],
         jnp.broadcast_to(table[..., -1:], lead + (d_max - hi,))], axis=-1)
    length = n_q + n_k
    rev = ext[..., ::-1]
    flat = jnp.tile(rev, (1,) * len(lead) + (n_q,))[..., :n_q * (length - 1)]
    toep = flat.reshape(lead + (n_q, length - 1))
    return toep[..., n_q - 1:n_q - 1 + n_k]


def _band_mask(tq, nkb, n_prev):
    r = np.arange(tq)[:, None]
    s = np.arange(nkb * tq)[None, :]
    q_chunk = (nkb - 1) * (tq // CHUNK) + r // CHUNK
    k_chunk = s // CHUNK
    return (k_chunk <= q_chunk) & (k_chunk >= q_chunk - n_prev)


def _attn_sample_kernel(*refs, n_heads, group, has_bias, has_sink, layer):
    q_ref, kn_ref, vn_ref, kc_ref, vc_ref = refs[:5]
    rest = list(refs[5:-1])
    bias_ref = rest.pop(0) if has_bias else None
    sink_ref = rest.pop(0) if has_sink else None
    o_ref = refs[-1]
    rows = kc_ref.shape[2]

    def pieces(c_ref, n_ref, s):
        cached = jnp.concatenate([c_ref[2 * s], c_ref[2 * s + 1]], axis=0).astype(BF)
        return [(cached, True), (n_ref[:, s * LANES:(s + 1) * LANES], False)]

    def k_of(s):
        return pieces(kc_ref, kn_ref, s)

    def v_of(s):
        return pieces(vc_ref, vn_ref, s)

    def bias_of(h0, nh):
        if not has_bias:
            return [0.0, 0.0]
        return [bias_ref[h0:h0 + nh, :, :rows], bias_ref[h0:h0 + nh, :, rows:]]

    sink_of = (lambda h: sink_ref[layer, h]) if has_sink else None
    _attend(q_ref, k_of, v_of, bias_of, sink_of, o_ref, n_heads=n_heads, group=group)


def _attn_sample(q_arr, q_blk, k_arr, k_blk, v_arr, v_blk, k_cache, v_cache, bias, sinks, l, *,
                 t, n_heads, group, name):
    _, bsz, kv_heads, hd, rows = k_cache.shape
    qw, kw = n_heads * HEAD_DIM, kv_heads * hd
    in_specs = [pl.BlockSpec((t, qw), lambda i: (i, q_blk)),
                pl.BlockSpec((t, kw), lambda i: (i, k_blk)),
                pl.BlockSpec((t, kw), lambda i: (i, v_blk)),
                pl.BlockSpec((None, None, kv_heads, hd, rows), lambda i: (l, i, 0, 0, 0)),
                pl.BlockSpec((None, None, kv_heads, hd, rows), lambda i: (l, i, 0, 0, 0))]
    args = [q_arr, k_arr, v_arr, k_cache, v_cache]
    if bias is not None:
        in_specs.append(pl.BlockSpec((None,) + bias.shape[1:], lambda i: (l, 0, 0, 0)))
        args.append(bias)
    if sinks is not None:
        in_specs.append(pl.BlockSpec(memory_space=pltpu.SMEM))
        args.append(sinks)
    return pl.pallas_call(
        functools.partial(_attn_sample_kernel, n_heads=n_heads, group=group,
                          has_bias=bias is not None, has_sink=sinks is not None, layer=l),
        grid=(bsz,), in_specs=in_specs,
        out_specs=pl.BlockSpec((t, qw), lambda i: (i, 0)),
        out_shape=jax.ShapeDtypeStruct((bsz * t, qw), BF),
        compiler_params=_params("parallel"),
        name=name,
    )(*args)


def _merge_kernel(*refs, n_branch):
    br = refs[:n_branch]
    gt = refs[n_branch:2 * n_branch]
    wb = refs[2 * n_branch:3 * n_branch]
    o_ref = refs[-1]
    acc = None
    for b, g, w in zip(br, gt, wb):
        y = g[...].astype(F32) * jnp.dot(b[...], w[...], preferred_element_type=F32)
        acc = y if acc is None else acc + y
    o_ref[...] = acc.astype(o_ref.dtype)


def _merge(branches, gates, w_branch, l, *, name, bm=512, bn=1024):
    n_branch = len(branches)
    m, kdim = branches[0].shape
    d = w_branch.shape[3]
    bm, bn = min(bm, m), min(bn, d)
    nj = d // bn
    in_specs = [pl.BlockSpec((bm, kdim), lambda j, i: (i, 0)) for _ in range(n_branch)]
    in_specs += [pl.BlockSpec((bm, bn), lambda j, i, b=b: (i, b * nj + j)) for b in range(n_branch)]
    in_specs += [pl.BlockSpec((None, None, kdim, bn), lambda j, i, b=b: (l, b, 0, j)) for b in range(n_branch)]
    return pl.pallas_call(
        functools.partial(_merge_kernel, n_branch=n_branch),
        grid=(nj, m // bm), in_specs=in_specs,
        out_specs=pl.BlockSpec((bm, bn), lambda j, i: (i, j)),
        out_shape=jax.ShapeDtypeStruct((m, d), BF),
        compiler_params=_params("parallel", "arbitrary"),
        name=name,
    )(*branches, *([gates] * n_branch), *([w_branch] * n_branch))


def _roll_kernel(c_ref, n_ref, o_ref, *, t):
    heads, hd, rows = c_ref.shape
    c = c_ref[...].reshape(heads * hd, rows)
    kept = pltpu.roll(c, rows - t, axis=1)
    tail = rows - LANES
    lane = lax.broadcasted_iota(jnp.int32, (heads * hd, LANES), 1)
    new = n_ref[...].reshape(heads * hd, LANES)
    last = jnp.where(lane >= LANES - t, new, kept[:, tail:])
    if tail:
        o_ref[:, :, 0:tail] = kept[:, :tail].reshape(heads, hd, tail)
    o_ref[:, :, tail:rows] = last.reshape(heads, hd, LANES)


def _roll_cache(cache, new, *, name):
    depth, bsz, rows, heads, hd = cache.shape
    t = new.shape[2]
    assert rows % LANES == 0 and t <= LANES
    to_device_order = lambda a: jnp.transpose(a, (0, 1, 3, 4, 2))
    new_t = jnp.pad(to_device_order(new), ((0, 0),) * 4 + ((LANES - t, 0),))
    blk = lambda n: pl.BlockSpec((None, None, heads, hd, n), lambda l, b: (l, b, 0, 0, 0))
    out = pl.pallas_call(
        functools.partial(_roll_kernel, t=t), grid=(depth, bsz),
        in_specs=[blk(rows), blk(LANES)], out_specs=blk(rows),
        out_shape=jax.ShapeDtypeStruct((depth, bsz, heads, hd, rows), cache.dtype),
        compiler_params=_params("parallel", "parallel"),
        name=name,
    )(to_device_order(cache), new_t)
    return jnp.transpose(out, (0, 1, 4, 2, 3))


def _run_path(tag, x3, mods, caches, pos, w, p):
    bsz, t, d = x3.shape
    m = bsz * t
    depth = w["in"].shape[0]
    br = d // 2
    n_heads = br // HEAD_DIM
    kvw = SWA_KV_HEADS * HEAD_DIM
    prompt = caches is None
    alpha = (2 * depth) ** 0.25
    col = w["cols"]

    def rowwise(a):
        return a if bsz == 1 else jnp.repeat(a, t, axis=0)

    def vec(a):
        return a[:, None, :]

    tables = _rope_tables(jnp.tile(pos, bsz))
    blk = min(t, GMLP_BLOCK)
    rep = GMLP_BLOCK // blk
    ws = jnp.tile(p["w_spatial"][:, :, :blk, :blk], (1, 1, rep, rep))
    bmap = jnp.repeat(jnp.swapaxes(jnp.tile(p["b_spatial"][:, :, :blk], (1, 1, rep)), 1, 2),
                      br // GMLP_GROUPS, axis=2)
    sinks = p["swa_sinks"]
    conv_w8 = jnp.broadcast_to(p["conv_w"][:, :, None, :], (depth, CONV_WIDTH, SUBLANES, br))
    if prompt:
        tq_c, nkb_c, tq_d, nkb_d = 128, 2, 128, 5
        per_c = 2 * n_heads // SWA_KV_HEADS
        bias_c = jnp.where(_band_mask(tq_c, nkb_c, SWA_PREV), 0.0, NEG_INF).astype(F32).T
        bias_c = jnp.broadcast_to(jnp.tile(bias_c, (1, per_c)), (depth, 1, nkb_c * tq_c, per_c * tq_c))
        bias_d = jnp.where(_band_mask(tq_d, nkb_d, BAND_PREV),
                           _rel_bias(p["band_rel_bias"], tq_d, nkb_d * tq_d, (nkb_d - 1) * tq_d).astype(F32),
                           NEG_INF)
        bias_d = bias_d.reshape(depth, n_heads // 2, 2 * tq_d, nkb_d * tq_d).swapaxes(2, 3)
    else:
        rows_d = caches[3].shape[2]
        bias_d = _rel_bias(p["band_rel_bias"], t, rows_d + t, rows_d).astype(F32)
        hist = jnp.pad(caches[0], ((0, 0), (0, 0), (CONV_HALO - (CONV_WIDTH - 1), 0), (0, 0)))
        hist = hist.reshape(depth, bsz * CONV_HALO, br)
        ck, cv, bk, bv = (jnp.transpose(c, (0, 1, 3, 4, 2)) for c in caches[1:])

    x = x3.reshape(m, d)
    h = _modulate(x, rowwise(mods[0][1]), rowwise(mods[0][0]), name=f"{tag}_mod0")
    states = []
    for l in range(depth):
        _, _, g1, sh2, sc2, g2 = mods[l]
        wide = dict(bm=2048, bn=512)
        ug = _mm(h, w["in"], l, col["gmlp"], 2 * br, act="gelu", name=f"{tag}_proj_gmlp", **wide)
        glu = _mm(h, w["in"], l, col["glu"], br, act="glu", bn=512, name=f"{tag}_proj_glu")
        swa = _mm(h, w["in"], l, col["swa"], br + 2 * kvw, name=f"{tag}_proj_swa", **wide)
        band = _mm(h, w["in"], l, col["band"], 3 * br, name=f"{tag}_proj_band", **wide)
        gates = _mm(h, w["in"], l, col["gate"], 4 * d, act="sigmoid", name=f"{tag}_proj_gate", **wide)
        conv_args = (conv_w8, vec(p["conv_b"]), vec(p["conv_ln_g"]), vec(p["conv_ln_b"]), l)
        if prompt:
            out_b = _conv(glu, glu, *conv_args, rows=256, prev_is_glu=True, name=f"{tag}_conv")
        else:
            out_b = _conv(glu, hist[l], *conv_args, rows=t, prev_is_glu=False, name=f"{tag}_conv")
        res = _gmlp(ug, ws, bmap, vec(p["gmlp_ln_g"]), vec(p["gmlp_ln_b"]), l,
                    blk=blk, emit_vn=not prompt, name=f"{tag}_gmlp")
        out_a = res[0]
        new_conv = glu.reshape(bsz, t, br)[:, t - (CONV_WIDTH - 1):].astype(F32)
        qk = _rope(swa, br + kvw, tables, name=f"{tag}_rope")
        new = lambda a, heads: a.astype(F32).reshape(bsz, -1, heads, HEAD_DIM)
        if prompt:
            out_c = _attn_prompt(qk, 0, qk, br // kvw, swa, (br + kvw) // kvw, bias_c, sinks, l,
                                 tq=tq_c, nkb=nkb_c, n_heads=n_heads, group=n_heads // SWA_KV_HEADS,
                                 name=f"{tag}_swa")
            out_d = _attn_prompt(band, 0, band, 1, band, 2, bias_d, None, l,
                                 tq=tq_d, nkb=nkb_d, n_heads=n_heads, group=1, name=f"{tag}_band")
            swa_k = new(qk[m - SWA_PREV * CHUNK:, br:], SWA_KV_HEADS)
            swa_v = new(swa[m - SWA_PREV * CHUNK:, br + kvw:], SWA_KV_HEADS)
            band_k = new(band[m - BAND_PREV * CHUNK:, br:2 * br], n_heads)
            band_v = new(band[m - BAND_PREV * CHUNK:, 2 * br:], n_heads)
        else:
            out_c = _attn_sample(qk, 0, qk, br // kvw, swa, (br + kvw) // kvw, ck, cv, None, sinks, l,
                                 t=t, n_heads=n_heads, group=n_heads // SWA_KV_HEADS, name=f"{tag}_swa")
            out_d = _attn_sample(band, 0, band, 1, band, 2, bk, bv, bias_d, None, l,
                                 t=t, n_heads=n_heads, group=1, name=f"{tag}_band")
            swa_k = new(qk[:, br:], SWA_KV_HEADS)
            swa_v = new(swa[:, br + kvw:], SWA_KV_HEADS)
            band_k = new(band[:, br:2 * br], n_heads)
            band_v = new(band[:, 2 * br:], n_heads)
        merged = _merge([out_a, out_b, out_c, out_d], gates, w["branch"], l, name=f"{tag}_merge")
        x, h2 = _mm_ln(merged, w["out"], l, x, g1, vec(p["ln1_g"]), vec(p["ln1_b"]), sc2, sh2, seg=t,
                       alpha=alpha, name=f"{tag}_out_ln")
        hid = _mm(h2, w["ff1"], l, 0, w["ff1"].shape[2], act="relu2", name=f"{tag}_ff1", **wide)
        nxt = mods[min(l + 1, depth - 1)]
        x, h = _mm_ln(hid, w["ff2"], l, x, g2, vec(p["ln2_g"]), vec(p["ln2_b"]),
                      nxt[1], nxt[0], seg=t, alpha=alpha, name=f"{tag}_ff2_ln")
        st = [new_conv, swa_k, swa_v, band_k, band_v]
        if not prompt:
            st.append(res[1].reshape(bsz, t, br))
        states.append(st)
    stacked = [jnp.stack(s, axis=0) for s in zip(*states)]
    if not prompt:
        for i in range(1, 5):
            stacked[i] = _roll_cache(caches[i], stacked[i], name=f"{tag}_roll{i}")
    return x.reshape(bsz, t, d), stacked


def _mxu_weights(w_ada, w_in, w_branch, w_out, w_ff1, w_ff2):
    br = w_branch.shape[2]
    kvw = SWA_KV_HEADS * HEAD_DIM
    cols, o = {}, 0
    for name, size in (("gmlp", 2 * br), ("glu", 2 * br), ("swa", br + 2 * kvw), ("band", 3 * br)):
        cols[name] = o
        o += size
    cols["gate"] = o
    cast = lambda a: a.astype(BF)
    return {"in": w_in, "cols": cols, "ada": w_ada, "branch": cast(w_branch),
            "out": cast(w_out), "ff1": w_ff1, "ff2": cast(w_ff2)}


def kernel(x_prompt, x_sample, state_conv, cache_swa_k, cache_swa_v, cache_band_k, cache_band_v, c_prompt, c_sample, w_ada, b_ada, w_in, gmlp_ln_g, gmlp_ln_b, w_spatial, b_spatial, conv_w, conv_b, conv_ln_g, conv_ln_b, swa_sinks, band_rel_bias, w_branch, w_out, ln1_g, ln1_b, w_ff1, w_ff2, ln2_g, ln2_b):
    depth = w_in.shape[0]
    d = x_prompt.shape[2]
    p = dict(gmlp_ln_g=gmlp_ln_g, gmlp_ln_b=gmlp_ln_b, w_spatial=w_spatial, b_spatial=b_spatial,
             conv_w=conv_w, conv_b=conv_b, conv_ln_g=conv_ln_g, conv_ln_b=conv_ln_b,
             swa_sinks=swa_sinks, band_rel_bias=band_rel_bias, ln1_g=ln1_g, ln1_b=ln1_b,
             ln2_g=ln2_g, ln2_b=ln2_b)
    w = _mxu_weights(w_ada, w_in, w_branch, w_out, w_ff1, w_ff2)
    nb_p, nb_s = c_prompt.shape[0], c_sample.shape[0]
    pad = (-(nb_p + nb_s)) % SUBLANES
    c_all = jnp.concatenate([c_prompt, c_sample, jnp.zeros((pad, d), F32)], axis=0)
    mods_p, mods_s = [], []
    for l in range(depth):
        mod = _mm(c_all, w["ada"], l, 0, w_ada.shape[2], pre="silu", bias=b_ada[:, None, :],
                  out_dtype=F32, name="adaln")
        parts = jnp.split(mod, 6, axis=-1)
        mods_p.append([a[:nb_p] for a in parts])
        mods_s.append([a[nb_p:nb_p + nb_s] for a in parts])
    pos_p = jnp.arange(x_prompt.shape[1], dtype=jnp.int32)
    pos_s = PAST_LEN + jnp.arange(x_sample.shape[1], dtype=jnp.int32)
    y_p, st_p = _run_path("p", x_prompt, mods_p, None, pos_p, w, p)
    y_s, st_s = _run_path("s", x_sample, mods_s,
                          (state_conv, cache_swa_k, cache_swa_v, cache_band_k, cache_band_v),
                          pos_s, w, p)
    conv_p, swa_k_p, swa_v_p, band_k_p, band_v_p = st_p
    conv_s, swa_k_s, swa_v_s, band_k_s, band_v_s, gmlp_v_s = st_s
    return (y_p, y_s, conv_p, conv_s, swa_k_p, swa_v_p, swa_k_s, swa_v_s,
            band_k_p, band_v_p, band_k_s, band_v_s, gmlp_v_s)
```

```python
import functools

import jax
import jax.numpy as jnp
import numpy as np
from jax import lax
from jax.experimental import pallas as pl
from jax.experimental.pallas import tpu as pltpu

BF = jnp.bfloat16
F32 = jnp.float32

CHUNK = 64
HEAD_DIM = 64
GMLP_BLOCK = 128
GMLP_GROUPS = 8
CONV_WIDTH = 31
SWA_KV_HEADS = 4
SWA_PREV = 2
BAND_PREV = 8
MAX_REL = 256
PAST_LEN = 1024
ROPE_THETA = 500000.0
ROT_DIM = HEAD_DIM // 4
LN_EPS = 1e-5
NEG_INF = -1e30

LANES = 128
SUBLANES = 8
CONV_HALO = 32
VMEM_LIMIT = 56 * 1024 * 1024


def _params(*sem):
    return pltpu.CompilerParams(dimension_semantics=sem, vmem_limit_bytes=VMEM_LIMIT)


def _ln(x, g, b):
    mu = jnp.mean(x, axis=-1, keepdims=True)
    xc = x - mu
    var = jnp.mean(xc * xc, axis=-1, keepdims=True)
    return xc * lax.rsqrt(var + LN_EPS) * g + b


def _sigmoid(x):
    return 0.5 * jnp.tanh(0.5 * x) + 0.5


def _mm_kernel(*refs, act, pre, has_bias):
    a_ref, o_ref = refs[0], refs[-1]
    a = a_ref[...]
    if pre == "silu":
        a = a * _sigmoid(a)
    a = a.astype(BF)
    acc = jnp.dot(a, refs[1][...].astype(BF), preferred_element_type=F32)
    if has_bias:
        acc = acc + refs[2][...]
    if act == "gelu":
        acc = jax.nn.gelu(acc)
    elif act == "sigmoid":
        acc = _sigmoid(acc)
    elif act == "relu2":
        acc = jnp.square(jnp.maximum(acc, 0.0))
    elif act == "glu":
        acc = acc * _sigmoid(jnp.dot(a, refs[2][...].astype(BF), preferred_element_type=F32))
    o_ref[...] = acc.astype(o_ref.dtype)


def _mm(a, w, l, col0, n, *, name, act=None, pre=None, bias=None, out_dtype=BF, bm=1024, bn=1024):
    m, k = a.shape
    bm, bn = min(bm, m), min(bn, n)
    assert m % bm == 0 and n % bn == 0 and col0 % bn == 0, (m, n, bm, bn, col0)
    c0 = col0 // bn
    in_specs = [pl.BlockSpec((bm, k), lambda i, j: (i, 0)),
                pl.BlockSpec((None, k, bn), lambda i, j: (l, 0, c0 + j))]
    args = [a, w]
    if act == "glu":
        g0 = (col0 + n) // bn
        in_specs.append(pl.BlockSpec((None, k, bn), lambda i, j: (l, 0, g0 + j)))
        args.append(w)
    if bias is not None:
        in_specs.append(pl.BlockSpec((None, 1, bn), lambda i, j: (l, 0, j)))
        args.append(bias)
    return pl.pallas_call(
        functools.partial(_mm_kernel, act=act, pre=pre, has_bias=bias is not None),
        grid=(m // bm, n // bn),
        in_specs=in_specs,
        out_specs=pl.BlockSpec((bm, bn), lambda i, j: (i, j)),
        out_shape=jax.ShapeDtypeStruct((m, n), out_dtype),
        compiler_params=_params("parallel", "arbitrary"),
        name=name,
    )(*args)


LN_ROWS = 128


def _mm_ln_kernel(a_ref, w_ref, x_ref, g_ref, lng_ref, lnb_ref, sc_ref, sh_ref,
                  xo_ref, ho_ref, acc0_ref, acc1_ref, *, alpha, nk, n_tiles, seg):
    i, k = pl.program_id(0), pl.program_id(1)
    step = min(LN_ROWS, seg)

    def mod(ref, r):
        if ref.shape[0] == 1:
            return ref[...]
        b = (r * step) // seg
        return ref[b:b + 1, :]

    def epilogue(done_ref):
        for r in range(xo_ref.shape[0] // step):
            rows = slice(r * step, (r + 1) * step)
            y = alpha * x_ref[rows, :] + (1.0 + mod(g_ref, r)) * done_ref[rows, :]
            xn = _ln(y, lng_ref[...], lnb_ref[...])
            xo_ref[rows, :] = xn
            ho_ref[rows, :] = (xn * (1.0 + mod(sc_ref, r)) + mod(sh_ref, r)).astype(ho_ref.dtype)

    def matmul(acc_ref, first):
        part = jnp.dot(a_ref[...], w_ref[...], preferred_element_type=F32)
        if first:
            acc_ref[...] = part
        else:
            acc_ref[...] += part

    for parity, (cur_ref, done_ref) in enumerate(((acc0_ref, acc1_ref), (acc1_ref, acc0_ref))):
        mine = (i % 2) == parity

        @pl.when(mine & (i > 0) & (i < n_tiles) & (k == 0))
        def _():
            matmul(cur_ref, True)
            epilogue(done_ref)

        @pl.when(mine & (i < n_tiles) & (k > 0))
        def _():
            matmul(cur_ref, False)

        if parity == 0:
            @pl.when((i == 0) & (k == 0))
            def _():
                matmul(cur_ref, True)

        if n_tiles % 2 == parity:
            @pl.when((i == n_tiles) & (k == 0))
            def _():
                epilogue(done_ref)


def _mm_ln(a, w, l, x, gate, ln_g, ln_b, scale, shift, *, alpha, seg, name, bm=512, bk=2048):
    m, kdim = a.shape
    d = w.shape[2]
    bm, bk = min(bm, m), min(bk, kdim)
    assert m % bm == 0 and kdim % bk == 0 and bm % min(LN_ROWS, seg) == 0
    n_tiles, nk = m // bm, kdim // bk
    last = n_tiles - 1
    k_of = lambda i, k: jnp.where(i > last, nk - 1, k)
    done = lambda i, k: (jnp.maximum(i - 1, 0), 0)
    const = pl.BlockSpec((None, 1, d), lambda i, k: (l, 0, 0))
    rows = pl.BlockSpec((bm, d), done)

    def mod_spec(arr):
        if arr.shape[0] == 1:
            return pl.BlockSpec((1, d), lambda i, k: (0, 0))
        assert arr.shape[0] * seg == m and bm % seg == 0
        return pl.BlockSpec((bm // seg, d), done)

    return pl.pallas_call(
        functools.partial(_mm_ln_kernel, alpha=alpha, nk=nk, n_tiles=n_tiles, seg=seg),
        grid=(n_tiles + 1, nk),
        in_specs=[pl.BlockSpec((bm, bk), lambda i, k: (jnp.minimum(i, last), k_of(i, k))),
                  pl.BlockSpec((None, bk, d), lambda i, k: (l, k_of(i, k), 0)),
                  rows, mod_spec(gate), const, const, mod_spec(scale), mod_spec(shift)],
        out_specs=[rows, rows],
        out_shape=[jax.ShapeDtypeStruct((m, d), F32), jax.ShapeDtypeStruct((m, d), BF)],
        scratch_shapes=[pltpu.VMEM((bm, d), F32), pltpu.VMEM((bm, d), F32)],
        compiler_params=_params("arbitrary", "arbitrary"),
        name=name,
    )(a, w, x, gate, ln_g, ln_b, scale, shift)


def _mod_kernel(x_ref, sc_ref, sh_ref, o_ref):
    o_ref[...] = (x_ref[...] * (1.0 + sc_ref[...]) + sh_ref[...]).astype(o_ref.dtype)


def _modulate(x, scale, shift, *, name, bm=512):
    m, d = x.shape
    bm = min(bm, m)
    rows = pl.BlockSpec((bm, d), lambda i: (i, 0))

    def spec(arr):
        if arr.shape[0] == 1:
            return pl.BlockSpec((1, d), lambda i: (0, 0))
        return rows

    return pl.pallas_call(
        _mod_kernel, grid=(m // bm,),
        in_specs=[rows, spec(scale), spec(shift)],
        out_specs=rows,
        out_shape=jax.ShapeDtypeStruct((m, d), BF),
        compiler_params=_params("parallel"),
        name=name,
    )(x, scale, shift)


def _rope_kernel(x_ref, c_ref, sa_ref, sb_ref, o_ref):
    half = ROT_DIM // 2
    c, sa, sb = c_ref[...], sa_ref[...], sb_ref[...]
    for s in range(x_ref.shape[1] // LANES):
        x = x_ref[:, s * LANES:(s + 1) * LANES].astype(F32)
        up = pltpu.roll(x, LANES - half, axis=1)
        dn = pltpu.roll(x, half, axis=1)
        o_ref[:, s * LANES:(s + 1) * LANES] = (x * c + up * sa + dn * sb).astype(o_ref.dtype)


def _rope(slab, width, tables, *, name, bm=512):
    m = slab.shape[0]
    bm = min(bm, m)
    tab = pl.BlockSpec((bm, LANES), lambda i: (i, 0))
    return pl.pallas_call(
        _rope_kernel, grid=(m // bm,),
        in_specs=[pl.BlockSpec((bm, width), lambda i: (i, 0)), tab, tab, tab],
        out_specs=pl.BlockSpec((bm, width), lambda i: (i, 0)),
        out_shape=jax.ShapeDtypeStruct((m, width), BF),
        compiler_params=_params("parallel"),
        name=name,
    )(slab, *tables)


def _rope_tables(pos):
    half = ROT_DIM // 2
    inv_freq = jnp.power(jnp.float32(ROPE_THETA), -jnp.arange(half, dtype=F32) / half)
    ang = pos.astype(F32)[:, None] * inv_freq[None, :]
    cos, sin = jnp.cos(ang), jnp.sin(ang)
    t = pos.shape[0]
    ones = jnp.ones((t, HEAD_DIM - ROT_DIM), F32)
    zeros = jnp.zeros((t, HEAD_DIM - ROT_DIM), F32)
    zh = jnp.zeros((t, half), F32)
    c = jnp.concatenate([cos, cos, ones], axis=1)
    sa = jnp.concatenate([-sin, zh, zeros], axis=1)
    sb = jnp.concatenate([zh, sin, zeros], axis=1)
    rep = LANES // HEAD_DIM
    return tuple(jnp.tile(a, (1, rep)) for a in (c, sa, sb))


def _gmlp_kernel(ug_ref, ws_ref, bmap_ref, lng_ref, lnb_ref, o_ref, *vn_refs, blk):
    width = o_ref.shape[1]
    gc = width // GMLP_GROUPS
    t = lax.broadcasted_iota(jnp.int32, (GMLP_BLOCK, GMLP_BLOCK), 0)
    s = lax.broadcasted_iota(jnp.int32, (GMLP_BLOCK, GMLP_BLOCK), 1)
    mask = (s <= t) & (s >= (t // blk) * blk)
    ws = [jnp.where(mask, ws_ref[g], 0.0).astype(BF) for g in range(GMLP_GROUPS)]
    bmap = bmap_ref[...]
    for r in range(o_ref.shape[0] // GMLP_BLOCK):
        rows = slice(r * GMLP_BLOCK, (r + 1) * GMLP_BLOCK)
        vn = _ln(ug_ref[rows, width:].astype(F32), lng_ref[...], lnb_ref[...])
        if vn_refs:
            vn_refs[0][rows, :] = vn
        vnb = vn.astype(BF)
        for g in range(GMLP_GROUPS):
            cols = slice(g * gc, (g + 1) * gc)
            mixed = jnp.dot(ws[g], vnb[:, cols], preferred_element_type=F32)
            u = ug_ref[rows, cols].astype(F32)
            o_ref[rows, cols] = (u * (mixed + bmap[:, cols])).astype(o_ref.dtype)


def _gmlp(ug, ws, bmap, ln_g, ln_b, l, *, blk, emit_vn, name, bm=512):
    m, two_w = ug.shape
    width = two_w // 2
    bm = min(bm, m)
    vec = pl.BlockSpec((None, 1, width), lambda i: (l, 0, 0))
    out_specs = [pl.BlockSpec((bm, width), lambda i: (i, 0))]
    out_shape = [jax.ShapeDtypeStruct((m, width), BF)]
    if emit_vn:
        out_specs.append(pl.BlockSpec((bm, width), lambda i: (i, 0)))
        out_shape.append(jax.ShapeDtypeStruct((m, width), F32))
    return pl.pallas_call(
        functools.partial(_gmlp_kernel, blk=blk), grid=(m // bm,),
        in_specs=[pl.BlockSpec((bm, two_w), lambda i: (i, 0)),
                  pl.BlockSpec((None, GMLP_GROUPS, GMLP_BLOCK, GMLP_BLOCK), lambda i: (l, 0, 0, 0)),
                  pl.BlockSpec((None, GMLP_BLOCK, width), lambda i: (l, 0, 0)),
                  vec, vec],
        out_specs=out_specs, out_shape=out_shape,
        compiler_params=_params("parallel"),
        name=name,
    )(ug, ws, bmap, ln_g, ln_b)


CONV_ROWS = 32
CONV_COPY_ROWS = 56


def _conv_body(cur_ref, prev_ref, w_ref, cb_ref, lng_ref, lnb_ref, o_ref, xs_ref, no_history):
    rows = cur_ref.shape[0]
    prev = prev_ref[...].astype(F32)
    if no_history is not None:
        prev = jnp.where(no_history, 0.0, prev)
    xs_ref[0, 0:CONV_HALO, :] = prev
    xs_ref[0, CONV_HALO:CONV_HALO + rows, :] = cur_ref[...].astype(F32)
    n_shift = CONV_HALO + rows - SUBLANES
    piece = max(p for p in range(SUBLANES, CONV_COPY_ROWS + 1, SUBLANES) if n_shift % p == 0)
    for b in range(1, SUBLANES):
        for r0 in range(0, n_shift, piece):
            xs_ref[b, r0:r0 + piece, :] = xs_ref[0, r0 + b:r0 + b + piece, :]
    first = CONV_HALO - (CONV_WIDTH - 1)
    groups = CONV_ROWS // SUBLANES
    for c in range(rows // CONV_ROWS):
        accs = [jnp.zeros((SUBLANES, cur_ref.shape[1]), F32) for _ in range(groups)]
        for j in range(CONV_WIDTH):
            b = (first + j) % SUBLANES
            r0 = c * CONV_ROWS + first + j - b
            w = w_ref[j]
            for g in range(groups):
                rg = r0 + g * SUBLANES
                accs[g] = accs[g] + w * xs_ref[b, rg:rg + SUBLANES, :]
        acc = jnp.concatenate(accs, axis=0)
        y = _ln(acc + cb_ref[...], lng_ref[...], lnb_ref[...])
        o_ref[c * CONV_ROWS:(c + 1) * CONV_ROWS, :] = (y * _sigmoid(y)).astype(o_ref.dtype)


def _conv_kernel(*refs, zero_first):
    _conv_body(*refs, (pl.program_id(0) == 0) if zero_first else None)


def _conv(glu, prev, conv_w, conv_b, ln_g, ln_b, l, *, rows, prev_is_glu, name):
    m, width = glu.shape
    rows = min(rows, m)
    step = rows // CONV_HALO
    if prev_is_glu:
        prev_spec = pl.BlockSpec((CONV_HALO, width), lambda i: (jnp.maximum(i * step - 1, 0), 0))
    else:
        prev_spec = pl.BlockSpec((CONV_HALO, width), lambda i: (i, 0))
    vec = pl.BlockSpec((None, 1, width), lambda i: (l, 0, 0))
    return pl.pallas_call(
        functools.partial(_conv_kernel, zero_first=prev_is_glu), grid=(m // rows,),
        in_specs=[pl.BlockSpec((rows, width), lambda i: (i, 0)), prev_spec,
                  pl.BlockSpec((None, CONV_WIDTH, SUBLANES, width), lambda i: (l, 0, 0, 0)),
                  vec, vec, vec],
        out_specs=pl.BlockSpec((rows, width), lambda i: (i, 0)),
        out_shape=jax.ShapeDtypeStruct((m, width), BF),
        scratch_shapes=[pltpu.VMEM((SUBLANES, CONV_HALO + rows, width), F32)],
        compiler_params=_params("parallel"),
        name=name,
    )(glu, prev, conv_w, conv_b, ln_g, ln_b)


def _half_mask(shape, half):
    lane = lax.broadcasted_iota(jnp.int32, shape, len(shape) - 1)
    return (lane >= HEAD_DIM) if half else (lane < HEAD_DIM)


def _stack_queries(q_ref, h0, per_slab, group):
    scale = HEAD_DIM ** -0.5
    blocks = []
    for qs in range(h0 // 2, (h0 + per_slab) // 2):
        q = (q_ref[:, qs * LANES:(qs + 1) * LANES].astype(F32) * scale).astype(BF)
        for q_half in range(2):
            k_half = ((2 * qs + q_half) // group) % 2
            qh = q if q_half == k_half else pltpu.roll(q, HEAD_DIM, axis=1)
            blocks.append(jnp.where(_half_mask(qh.shape, k_half), qh, jnp.zeros_like(qh)))
    return jnp.concatenate(blocks, axis=0)


def _attend_cols(q_ref, k_of, v_of, bias_of, sink_of, o_ref, *, n_heads, group):
    tq = q_ref.shape[0]
    per_slab = 2 * group
    n_slabs = n_heads // per_slab

    def scores(ks):
        qst = _stack_queries(q_ref, ks * per_slab, per_slab, group)
        s = lax.dot_general(k_of(ks), qst, (((1,), (1,)), ((), ())), preferred_element_type=F32)
        return bias_of(ks, s)

    s_next = scores(0)
    for ks in range(n_slabs):
        h0 = ks * per_slab
        s = s_next
        if ks + 1 < n_slabs:
            s_next = scores(ks + 1)
        m = jnp.max(s, axis=0, keepdims=True)
        if sink_of is not None:
            sink = jnp.concatenate([jnp.full((1, tq), sink_of(h), F32)
                                    for h in range(h0, h0 + per_slab)], axis=1)
            m = jnp.maximum(m, sink)
        p = jnp.exp(s - m)
        den = jnp.sum(p, axis=0, keepdims=True)
        if sink_of is not None:
            den = den + jnp.exp(sink - m)
        ot = lax.dot_general(v_of(ks), p.astype(BF), (((0,), (0,)), ((), ())),
                             preferred_element_type=F32)
        ot = ot / den
        for qs in range(h0 // 2, (h0 + per_slab) // 2):
            parts = []
            for q_half in range(2):
                h = 2 * qs + q_half
                k_half = (h // group) % 2
                parts.append(ot[k_half * HEAD_DIM:(k_half + 1) * HEAD_DIM, (h - h0) * tq:(h - h0 + 1) * tq])
            o_ref[:, qs * LANES:(qs + 1) * LANES] = jnp.concatenate(parts, axis=0).T.astype(o_ref.dtype)


def _attend(q_ref, k_of, v_of, bias_of, sink_of, o_ref, *, n_heads, group):
    tq = q_ref.shape[0]
    per_slab = 2 * group
    contract_last = (((1,), (1,)), ((), ()))
    n_slabs = n_heads // per_slab

    def scores_of(ks):
        qst = _stack_queries(q_ref, ks * per_slab, per_slab, group)
        scores = [jnp.dot(qst, kp, preferred_element_type=F32) if transposed else
                  lax.dot_general(qst, kp, contract_last, preferred_element_type=F32)
                  for kp, transposed in k_of(ks)]
        return [s.reshape(per_slab, tq, s.shape[1]) + b
                for s, b in zip(scores, bias_of(ks * per_slab, per_slab))]

    next_scores = scores_of(0)
    for ks in range(n_slabs):
        h0 = ks * per_slab
        scores = next_scores
        if ks + 1 < n_slabs:
            next_scores = scores_of(ks + 1)
        m = functools.reduce(jnp.maximum, [jnp.max(s, axis=-1, keepdims=True) for s in scores])
        if sink_of is not None:
            sink = jnp.concatenate([jnp.full((1, tq, 1), sink_of(h), F32)
                                    for h in range(h0, h0 + per_slab)], axis=0)
            m = jnp.maximum(m, sink)
        ps = [jnp.exp(s - m) for s in scores]
        den = functools.reduce(lambda a, b: a + b, [jnp.sum(p, axis=-1, keepdims=True) for p in ps])
        if sink_of is not None:
            den = den + jnp.exp(sink - m)
        pbs = [p.reshape(per_slab * tq, p.shape[2]).astype(BF) for p in ps]
        o = functools.reduce(
            lambda a, b: a + b,
            [lax.dot_general(pb, vp, contract_last, preferred_element_type=F32) if transposed else
             jnp.dot(pb, vp, preferred_element_type=F32)
             for pb, (vp, transposed) in zip(pbs, v_of(ks))])
        o = o.reshape(per_slab, tq, LANES) / den
        for qs in range(h0 // 2, (h0 + per_slab) // 2):
            halves = []
            for q_half in range(2):
                h = 2 * qs + q_half
                oh = o[h - h0]
                if (h // group) % 2 != q_half:
                    oh = pltpu.roll(oh, HEAD_DIM, axis=1)
                halves.append(oh)
            out = jnp.where(_half_mask(halves[0].shape, 0), halves[0], halves[1])
            o_ref[:, qs * LANES:(qs + 1) * LANES] = out.astype(o_ref.dtype)


def _attn_prompt_kernel(*refs, nkb, n_heads, group, has_sink, layer):
    q_ref = refs[0]
    k_refs = refs[1:1 + nkb]
    v_refs = refs[1 + nkb:1 + 2 * nkb]
    bias_ref = refs[1 + 2 * nkb]
    sink_ref = refs[2 + 2 * nkb] if has_sink else None
    o_ref = refs[-1]
    tq = q_ref.shape[0]
    first_key = (nkb - 1 - pl.program_id(0)) * tq

    def k_of(s):
        return jnp.concatenate([r[:, s * LANES:(s + 1) * LANES] for r in k_refs], axis=0)

    def v_of(s):
        return jnp.concatenate([r[:, s * LANES:(s + 1) * LANES] for r in v_refs], axis=0)

    def bias_of(ks, s):
        return s + (bias_ref[ks] if bias_ref.shape[0] > 1 else bias_ref[0])

    def bias_at_start(ks, s):
        key = lax.broadcasted_iota(jnp.int32, s.shape, 0)
        return jnp.where(key < first_key, NEG_INF, bias_of(ks, s))

    sink_of = (lambda h: sink_ref[layer, h]) if has_sink else None

    @pl.when(first_key > 0)
    def _():
        _attend_cols(q_ref, k_of, v_of, bias_at_start, sink_of, o_ref, n_heads=n_heads, group=group)

    @pl.when(first_key <= 0)
    def _():
        _attend_cols(q_ref, k_of, v_of, bias_of, sink_of, o_ref, n_heads=n_heads, group=group)


def _attn_prompt(q_arr, q_blk, k_arr, k_blk, v_arr, v_blk, bias, sinks, l, *, tq, nkb, n_heads, group, name):
    m = q_arr.shape[0]
    qw = n_heads * HEAD_DIM
    kw = (n_heads // group) * HEAD_DIM
    in_specs = [pl.BlockSpec((tq, qw), lambda i: (i, q_blk))]
    args = [q_arr]
    for arr, blk in ((k_arr, k_blk), (v_arr, v_blk)):
        for b in range(nkb):
            in_specs.append(pl.BlockSpec(
                (tq, kw), lambda i, b=b, blk=blk: (jnp.maximum(i - (nkb - 1) + b, 0), blk)))
            args.append(arr)
    in_specs.append(pl.BlockSpec((None,) + bias.shape[1:], lambda i: (l, 0, 0, 0)))
    args.append(bias)
    if sinks is not None:
        in_specs.append(pl.BlockSpec(memory_space=pltpu.SMEM))
        args.append(sinks)
    return pl.pallas_call(
        functools.partial(_attn_prompt_kernel, nkb=nkb, n_heads=n_heads, group=group,
                          has_sink=sinks is not None, layer=l),
        grid=(m // tq,), in_specs=in_specs,
        out_specs=pl.BlockSpec((tq, qw), lambda i: (i, 0)),
        out_shape=jax.ShapeDtypeStruct((m, qw), BF),
        compiler_params=_params("parallel"),
        name=name,
    )(*args)


def _rel_bias(table, n_q, n_k, offset):
    d_min, d_max = offset - (n_k - 1), offset + n_q - 1
    lo, hi = max(d_min, -MAX_REL), min(d_max, MAX_REL)
    lead = table.shape[:-1]
    ext = jnp.concatenate(
        [jnp.broadcast_to(table[..., :1], lead + (lo - d_min + 1,)),
         table[..., lo + MAX_REL:hi + MAX_REL + 1],
         jnp.broadcast_to(table[..., -1:], lead + (d_max - hi,))], axis=-1)
    length = n_q + n_k
    rev = ext[..., ::-1]
    flat = jnp.tile(rev, (1,) * len(lead) + (n_q,))[..., :n_q * (length - 1)]
    toep = flat.reshape(lead + (n_q, length - 1))
    return toep[..., n_q - 1:n_q - 1 + n_k]


def _band_mask(tq, nkb, n_prev):
    r = np.arange(tq)[:, None]
    s = np.arange(nkb * tq)[None, :]
    q_chunk = (nkb - 1) * (tq // CHUNK) + r // CHUNK
    k_chunk = s // CHUNK
    return (k_chunk <= q_chunk) & (k_chunk >= q_chunk - n_prev)


def _attn_sample_kernel(*refs, n_heads, group, has_bias, has_sink, layer):
    q_ref, kn_ref, vn_ref, kc_ref, vc_ref = refs[:5]
    rest = list(refs[5:-1])
    bias_ref = rest.pop(0) if has_bias else None
    sink_ref = rest.pop(0) if has_sink else None
    o_ref = refs[-1]
    rows = kc_ref.shape[2]

    def pieces(c_ref, n_ref, s):
        cached = jnp.concatenate([c_ref[2 * s], c_ref[2 * s + 1]], axis=0).astype(BF)
        return [(cached, True), (n_ref[:, s * LANES:(s + 1) * LANES], False)]

    def k_of(s):
        return pieces(kc_ref, kn_ref, s)

    def v_of(s):
        return pieces(vc_ref, vn_ref, s)

    def bias_of(h0, nh):
        if not has_bias:
            return [0.0, 0.0]
        return [bias_ref[h0:h0 + nh, :, :rows], bias_ref[h0:h0 + nh, :, rows:]]

    sink_of = (lambda h: sink_ref[layer, h]) if has_sink else None
    _attend(q_ref, k_of, v_of, bias_of, sink_of, o_ref, n_heads=n_heads, group=group)


def _attn_sample(q_arr, q_blk, k_arr, k_blk, v_arr, v_blk, k_cache, v_cache, bias, sinks, l, *,
                 t, n_heads, group, name):
    _, bsz, kv_heads, hd, rows = k_cache.shape
    qw, kw = n_heads * HEAD_DIM, kv_heads * hd
    in_specs = [pl.BlockSpec((t, qw), lambda i: (i, q_blk)),
                pl.BlockSpec((t, kw), lambda i: (i, k_blk)),
                pl.BlockSpec((t, kw), lambda i: (i, v_blk)),
                pl.BlockSpec((None, None, kv_heads, hd, rows), lambda i: (l, i, 0, 0, 0)),
                pl.BlockSpec((None, None, kv_heads, hd, rows), lambda i: (l, i, 0, 0, 0))]
    args = [q_arr, k_arr, v_arr, k_cache, v_cache]
    if bias is not None:
        in_specs.append(pl.BlockSpec((None,) + bias.shape[1:], lambda i: (l, 0, 0, 0)))
        args.append(bias)
    if sinks is not None:
        in_specs.append(pl.BlockSpec(memory_space=pltpu.SMEM))
        args.append(sinks)
    return pl.pallas_call(
        functools.partial(_attn_sample_kernel, n_heads=n_heads, group=group,
                          has_bias=bias is not None, has_sink=sinks is not None, layer=l),
        grid=(bsz,), in_specs=in_specs,
        out_specs=pl.BlockSpec((t, qw), lambda i: (i, 0)),
        out_shape=jax.ShapeDtypeStruct((bsz * t, qw), BF),
        compiler_params=_params("parallel"),
        name=name,
    )(*args)


def _merge_kernel(*refs, n_branch):
    br = refs[:n_branch]
    gt = refs[n_branch:2 * n_branch]
    wb = refs[2 * n_branch:3 * n_branch]
    o_ref = refs[-1]
    acc = None
    for b, g, w in zip(br, gt, wb):
        y = g[...].astype(F32) * jnp.dot(b[...], w[...], preferred_element_type=F32)
        acc = y if acc is None else acc + y
    o_ref[...] = acc.astype(o_ref.dtype)


def _merge(branches, gates, w_branch, l, *, name, bm=512, bn=1024):
    n_branch = len(branches)
    m, kdim = branches[0].shape
    d = w_branch.shape[3]
    bm, bn = min(bm, m), min(bn, d)
    nj = d // bn
    in_specs = [pl.BlockSpec((bm, kdim), lambda j, i: (i, 0)) for _ in range(n_branch)]
    in_specs += [pl.BlockSpec((bm, bn), lambda j, i, b=b: (i, b * nj + j)) for b in range(n_branch)]
    in_specs += [pl.BlockSpec((None, None, kdim, bn), lambda j, i, b=b: (l, b, 0, j)) for b in range(n_branch)]
    return pl.pallas_call(
        functools.partial(_merge_kernel, n_branch=n_branch),
        grid=(nj, m // bm), in_specs=in_specs,
        out_specs=pl.BlockSpec((bm, bn), lambda j, i: (i, j)),
        out_shape=jax.ShapeDtypeStruct((m, d), BF),
        compiler_params=_params("parallel", "arbitrary"),
        name=name,
    )(*branches, *([gates] * n_branch), *([w_branch] * n_branch))


def _roll_kernel(c_ref, n_ref, o_ref, *, t):
    heads, hd, rows = c_ref.shape
    c = c_ref[...].reshape(heads * hd, rows)
    kept = pltpu.roll(c, rows - t, axis=1)
    tail = rows - LANES
    lane = lax.broadcasted_iota(jnp.int32, (heads * hd, LANES), 1)
    new = n_ref[...].reshape(heads * hd, LANES)
    last = jnp.where(lane >= LANES - t, new, kept[:, tail:])
    if tail:
        o_ref[:, :, 0:tail] = kept[:, :tail].reshape(heads, hd, tail)
    o_ref[:, :, tail:rows] = last.reshape(heads, hd, LANES)


def _roll_cache(cache, new, *, name):
    depth, bsz, rows, heads, hd = cache.shape
    t = new.shape[2]
    assert rows % LANES == 0 and t <= LANES
    to_device_order = lambda a: jnp.transpose(a, (0, 1, 3, 4, 2))
    new_t = jnp.pad(to_device_order(new), ((0, 0),) * 4 + ((LANES - t, 0),))
    blk = lambda n: pl.BlockSpec((None, None, heads, hd, n), lambda l, b: (l, b, 0, 0, 0))
    out = pl.pallas_call(
        functools.partial(_roll_kernel, t=t), grid=(depth, bsz),
        in_specs=[blk(rows), blk(LANES)], out_specs=blk(rows),
        out_shape=jax.ShapeDtypeStruct((depth, bsz, heads, hd, rows), cache.dtype),
        compiler_params=_params("parallel", "parallel"),
        name=name,
    )(to_device_order(cache), new_t)
    return jnp.transpose(out, (0, 1, 4, 2, 3))


def _run_path(tag, x3, mods, caches, pos, w, p):
    bsz, t, d = x3.shape
    m = bsz * t
    depth = w["in"].shape[0]
    br = d // 2
    n_heads = br // HEAD_DIM
    kvw = SWA_KV_HEADS * HEAD_DIM
    prompt = caches is None
    alpha = (2 * depth) ** 0.25
    col = w["cols"]

    def rowwise(a):
        return a if bsz == 1 else jnp.repeat(a, t, axis=0)

    def vec(a):
        return a[:, None, :]

    tables = _rope_tables(jnp.tile(pos, bsz))
    blk = min(t, GMLP_BLOCK)
    rep = GMLP_BLOCK // blk
    ws = jnp.tile(p["w_spatial"][:, :, :blk, :blk], (1, 1, rep, rep))
    bmap = jnp.repeat(jnp.swapaxes(jnp.tile(p["b_spatial"][:, :, :blk], (1, 1, rep)), 1, 2),
                      br // GMLP_GROUPS, axis=2)
    sinks = p["swa_sinks"]
    conv_w8 = jnp.broadcast_to(p["conv_w"][:, :, None, :], (depth, CONV_WIDTH, SUBLANES, br))
    if prompt:
        tq_c, nkb_c, tq_d, nkb_d = 128, 2, 128, 5
        per_c = 2 * n_heads // SWA_KV_HEADS
        bias_c = jnp.where(_band_mask(tq_c, nkb_c, SWA_PREV), 0.0, NEG_INF).astype(F32).T
        bias_c = jnp.broadcast_to(jnp.tile(bias_c, (1, per_c)), (depth, 1, nkb_c * tq_c, per_c * tq_c))
        bias_d = jnp.where(_band_mask(tq_d, nkb_d, BAND_PREV),
                           _rel_bias(p["band_rel_bias"], tq_d, nkb_d * tq_d, (nkb_d - 1) * tq_d).astype(F32),
                           NEG_INF)
        bias_d = bias_d.reshape(depth, n_heads // 2, 2 * tq_d, nkb_d * tq_d).swapaxes(2, 3)
    else:
        rows_d = caches[3].shape[2]
        bias_d = _rel_bias(p["band_rel_bias"], t, rows_d + t, rows_d).astype(F32)
        hist = jnp.pad(caches[0], ((0, 0), (0, 0), (CONV_HALO - (CONV_WIDTH - 1), 0), (0, 0)))
        hist = hist.reshape(depth, bsz * CONV_HALO, br)
        ck, cv, bk, bv = (jnp.transpose(c, (0, 1, 3, 4, 2)) for c in caches[1:])

    x = x3.reshape(m, d)
    h = _modulate(x, rowwise(mods[0][1]), rowwise(mods[0][0]), name=f"{tag}_mod0")
    states = []
    for l in range(depth):
        _, _, g1, sh2, sc2, g2 = mods[l]
        wide = dict(bm=2048, bn=512)
        ug = _mm(h, w["in"], l, col["gmlp"], 2 * br, act="gelu", name=f"{tag}_proj_gmlp", **wide)
        glu = _mm(h, w["in"], l, col["glu"], br, act="glu", bn=512, name=f"{tag}_proj_glu")
        swa = _mm(h, w["in"], l, col["swa"], br + 2 * kvw, name=f"{tag}_proj_swa", **wide)
        band = _mm(h, w["in"], l, col["band"], 3 * br, name=f"{tag}_proj_band", **wide)
        gates = _mm(h, w["in"], l, col["gate"], 4 * d, act="sigmoid", name=f"{tag}_proj_gate", **wide)
        conv_args = (conv_w8, vec(p["conv_b"]), vec(p["conv_ln_g"]), vec(p["conv_ln_b"]), l)
        if prompt:
            out_b = _conv(glu, glu, *conv_args, rows=256, prev_is_glu=True, name=f"{tag}_conv")
        else:
            out_b = _conv(glu, hist[l], *conv_args, rows=t, prev_is_glu=False, name=f"{tag}_conv")
        res = _gmlp(ug, ws, bmap, vec(p["gmlp_ln_g"]), vec(p["gmlp_ln_b"]), l,
                    blk=blk, emit_vn=not prompt, name=f"{tag}_gmlp")
        out_a = res[0]
        new_conv = glu.reshape(bsz, t, br)[:, t - (CONV_WIDTH - 1):].astype(F32)
        qk = _rope(swa, br + kvw, tables, name=f"{tag}_rope")
        new = lambda a, heads: a.astype(F32).reshape(bsz, -1, heads, HEAD_DIM)
        if prompt:
            out_c = _attn_prompt(qk, 0, qk, br // kvw, swa, (br + kvw) // kvw, bias_c, sinks, l,
                                 tq=tq_c, nkb=nkb_c, n_heads=n_heads, group=n_heads // SWA_KV_HEADS,
                                 name=f"{tag}_swa")
            out_d = _attn_prompt(band, 0, band, 1, band, 2, bias_d, None, l,
                                 tq=tq_d, nkb=nkb_d, n_heads=n_heads, group=1, name=f"{tag}_band")
            swa_k = new(qk[m - SWA_PREV * CHUNK:, br:], SWA_KV_HEADS)
            swa_v = new(swa[m - SWA_PREV * CHUNK:, br + kvw:], SWA_KV_HEADS)
            band_k = new(band[m - BAND_PREV * CHUNK:, br:2 * br], n_heads)
            band_v = new(band[m - BAND_PREV * CHUNK:, 2 * br:], n_heads)
        else:
            out_c = _attn_sample(qk, 0, qk, br // kvw, swa, (br + kvw) // kvw, ck, cv, None, sinks, l,
                                 t=t, n_heads=n_heads, group=n_heads // SWA_KV_HEADS, name=f"{tag}_swa")
            out_d = _attn_sample(band, 0, band, 1, band, 2, bk, bv, bias_d, None, l,
                                 t=t, n_heads=n_heads, group=1, name=f"{tag}_band")
            swa_k = new(qk[:, br:], SWA_KV_HEADS)
            swa_v = new(swa[:, br + kvw:], SWA_KV_HEADS)
            band_k = new(band[:, br:2 * br], n_heads)
            band_v = new(band[:, 2 * br:], n_heads)
        merged = _merge([out_a, out_b, out_c, out_d], gates, w["branch"], l, name=f"{tag}_merge")
        x, h2 = _mm_ln(merged, w["out"], l, x, g1, vec(p["ln1_g"]), vec(p["ln1_b"]), sc2, sh2, seg=t,
                       alpha=alpha, name=f"{tag}_out_ln")
        hid = _mm(h2, w["ff1"], l, 0, w["ff1"].shape[2], act="relu2", name=f"{tag}_ff1", **wide)
        nxt = mods[min(l + 1, depth - 1)]
        x, h = _mm_ln(hid, w["ff2"], l, x, g2, vec(p["ln2_g"]), vec(p["ln2_b"]),
                      nxt[1], nxt[0], seg=t, alpha=alpha, name=f"{tag}_ff2_ln")
        st = [new_conv, swa_k, swa_v, band_k, band_v]
        if not prompt:
            st.append(res[1].reshape(bsz, t, br))
        states.append(st)
    stacked = [jnp.stack(s, axis=0) for s in zip(*states)]
    if not prompt:
        for i in range(1, 5):
            stacked[i] = _roll_cache(caches[i], stacked[i], name=f"{tag}_roll{i}")
    return x.reshape(bsz, t, d), stacked


def _mxu_weights(w_ada, w_in, w_branch, w_out, w_ff1, w_ff2):
    br = w_branch.shape[2]
    kvw = SWA_KV_HEADS * HEAD_DIM
    cols, o = {}, 0
    for name, size in (("gmlp", 2 * br), ("glu", 2 * br), ("swa", br + 2 * kvw), ("band", 3 * br)):
        cols[name] = o
        o += size
    cols["gate"] = o
    cast = lambda a: a.astype(BF)
    return {"in": w_in, "cols": cols, "ada": w_ada, "branch": cast(w_branch),
            "out": cast(w_out), "ff1": w_ff1, "ff2": cast(w_ff2)}


def kernel(x_prompt, x_sample, state_conv, cache_swa_k, cache_swa_v, cache_band_k, cache_band_v, c_prompt, c_sample, w_ada, b_ada, w_in, gmlp_ln_g, gmlp_ln_b, w_spatial, b_spatial, conv_w, conv_b, conv_ln_g, conv_ln_b, swa_sinks, band_rel_bias, w_branch, w_out, ln1_g, ln1_b, w_ff1, w_ff2, ln2_g, ln2_b):
    depth = w_in.shape[0]
    d = x_prompt.shape[2]
    p = dict(gmlp_ln_g=gmlp_ln_g, gmlp_ln_b=gmlp_ln_b, w_spatial=w_spatial, b_spatial=b_spatial,
             conv_w=conv_w, conv_b=conv_b, conv_ln_g=conv_ln_g, conv_ln_b=conv_ln_b,
             swa_sinks=swa_sinks, band_rel_bias=band_rel_bias, ln1_g=ln1_g, ln1_b=ln1_b,
             ln2_g=ln2_g, ln2_b=ln2_b)
    w = _mxu_weights(w_ada, w_in, w_branch, w_out, w_ff1, w_ff2)
    nb_p, nb_s = c_prompt.shape[0], c_sample.shape[0]
    pad = (-(nb_p + nb_s)) % SUBLANES
    c_all = jnp.concatenate([c_prompt, c_sample, jnp.zeros((pad, d), F32)], axis=0)
    mods_p, mods_s = [], []
    for l in range(depth):
        mod = _mm(c_all, w["ada"], l, 0, w_ada.shape[2], pre="silu", bias=b_ada[:, None, :],
                  out_dtype=F32, name="adaln")
        parts = jnp.split(mod, 6, axis=-1)
        mods_p.append([a[:nb_p] for a in parts])
        mods_s.append([a[nb_p:nb_p + nb_s] for a in parts])
    pos_p = jnp.arange(x_prompt.shape[1], dtype=jnp.int32)
    pos_s = PAST_LEN + jnp.arange(x_sample.shape[1], dtype=jnp.int32)
    y_p, st_p = _run_path("p", x_prompt, mods_p, None, pos_p, w, p)
    y_s, st_s = _run_path("s", x_sample, mods_s,
                          (state_conv, cache_swa_k, cache_swa_v, cache_band_k, cache_band_v),
                          pos_s, w, p)
    conv_p, swa_k_p, swa_v_p, band_k_p, band_v_p = st_p
    conv_s, swa_k_s, swa_v_s, band_k_s, band_v_s, gmlp_v_s = st_s
    return (y_p, y_s, conv_p, conv_s, swa_k_p, swa_v_p, swa_k_s, swa_v_s,
            band_k_p, band_v_p, band_k_s, band_v_s, gmlp_v_s)
```
